```python
import math
import jax, jax.numpy as jnp
from jax import lax
import numpy as np

D_MODEL = 1024
BATCH = 8
SEQ = 4096
DEPTH = 2
DEC_BATCH = 16
DEC_SEQ = 2048
PAST_LEN = 128

PLE_DIM = 256
N_MIXERS = 2
N_S5_LAYERS = (DEPTH + 1) // 2
N_RG_LAYERS = DEPTH // 2
S5_WIDTH = D_MODEL
S5_GROUP = 16
S5_GROUPS = S5_WIDTH // S5_GROUP
S5_STATE = 64
S5_DT_MIN = 1e-3
S5_DT_MAX = 1e-1
RG_WIDTH = D_MODEL
RG_BLOCKS = 4
RG_BLOCK = RG_WIDTH // RG_BLOCKS
RG_CONV = 4
RG_CONV_LEFT = 1
RG_C = 8.0
PEER_HEADS = 8
PEER_NKEYS = 128
PEER_EXPERTS = PEER_NKEYS * PEER_NKEYS
PEER_QDIM = 256
PEER_HALF = PEER_QDIM // 2
PEER_TOPK = 16
PEER_CHUNK = 512
ALPHA = (2 * DEPTH) ** 0.25
BETA = (8 * DEPTH) ** -0.25
LN_EPS = 1e-5

kernel_name = 'hybrid_s5_rglru_peer_encoder'

F32 = jnp.float32


def _layernorm(x, g, b):
    xf = x.astype(F32)
    mu = jnp.mean(xf, axis=-1, keepdims=True)
    var = jnp.mean(jnp.square(xf - mu), axis=-1, keepdims=True)
    return ((xf - mu) * lax.rsqrt(var + LN_EPS) * g.astype(F32) + b.astype(F32)).astype(x.dtype)


def _cplx_combine(left, right):
    a1r, a1i, b1r, b1i = left
    a2r, a2i, b2r, b2i = right
    return (a1r * a2r - a1i * a2i,
            a1r * a2i + a1i * a2r,
            a2r * b1r - a2i * b1i + b2r,
            a2r * b1i + a2i * b1r + b2i)


def _real_combine(left, right):
    a1, b1 = left
    a2, b2 = right
    return (a1 * a2, a2 * b1 + b2)


def _s5_direction(ug, lam_re, lam_im, log_step, b_re, b_im, c_re, c_im, reverse):
    L = ug.shape[1]
    step = jnp.exp(log_step.astype(F32))[:, None]
    lr = lam_re.astype(F32)
    li = lam_im.astype(F32)
    mag = jnp.exp(lr * step)
    ang = li * step
    ar = mag * jnp.cos(ang)
    ai = mag * jnp.sin(ang)
    den = lr * lr + li * li
    zr = ar - 1.0
    qr = (zr * lr + ai * li) / den
    qi = (ai * lr - zr * li) / den
    br = b_re.astype(F32)
    bi = b_im.astype(F32)
    bbr = qr[..., None] * br - qi[..., None] * bi
    bbi = qr[..., None] * bi + qi[..., None] * br
    bu_r = jnp.einsum('blgc,gpc->blgp', ug, bbr)
    bu_i = jnp.einsum('blgc,gpc->blgp', ug, bbi)
    a_r = jnp.broadcast_to(ar, (1, L) + ar.shape)
    a_i = jnp.broadcast_to(ai, (1, L) + ai.shape)
    _, _, h_r, h_i = lax.associative_scan(_cplx_combine, (a_r, a_i, bu_r, bu_i), reverse=reverse, axis=1)
    return (jnp.einsum('blgp,gcp->blgc', h_r, c_re.astype(F32))
            - jnp.einsum('blgp,gcp->blgc', h_i, c_im.astype(F32)))


def _s5_mixer(x, w_in, lam_re, lam_im, log_step, b_re, b_im, c_re, c_im, d_skip, w_glu):
    bsz, L, _ = x.shape
    u = (x @ w_in).astype(F32)
    ug = u.reshape(bsz, L, S5_GROUPS, S5_GROUP)
    y = (_s5_direction(ug, lam_re[0], lam_im[0], log_step[0], b_re[0], b_im[0], c_re[0], c_im[0], False)
         + _s5_direction(ug, lam_re[1], lam_im[1], log_step[1], b_re[1], b_im[1], c_re[1], c_im[1], True))
    y = y.reshape(bsz, L, S5_WIDTH) + d_skip.astype(F32) * u
    h = jax.nn.gelu(y).astype(x.dtype)
    val, gate = jnp.split(h @ w_glu, 2, axis=-1)
    return val * jax.nn.sigmoid(gate)


def _rg_direction(cf, cb, w_a, b_a, w_x, b_x, lam, reverse):
    shape = cf.shape
    r_gate = jax.nn.sigmoid(jnp.einsum('blhi,hij->blhj', cb, w_a.astype(F32)).reshape(shape) + b_a.astype(F32))
    i_gate = jax.nn.sigmoid(jnp.einsum('blhi,hij->blhj', cb, w_x.astype(F32)).reshape(shape) + b_x.astype(F32))
    log_a = -RG_C * r_gate * jax.nn.softplus(-lam.astype(F32))
    a = jnp.exp(log_a)
    b = jnp.sqrt(-jnp.expm1(2.0 * log_a)) * (i_gate * cf)
    _, h = lax.associative_scan(_real_combine, (a, b), reverse=reverse, axis=1)
    return h


def _rg_mixer(x, w_in, conv_w, conv_b, w_ga, b_ga, w_gx, b_gx, lam, w_out):
    bsz, L, _ = x.shape
    g, r = jnp.split(x @ w_in, 2, axis=-1)
    rp = jnp.pad(r, ((0, 0), (RG_CONV_LEFT, RG_CONV - 1 - RG_CONV_LEFT), (0, 0)))
    c = conv_b + sum(rp[:, k:k + L] * conv_w[k] for k in range(RG_CONV))
    cf = c.astype(F32)
    cb = cf.reshape(bsz, L, RG_BLOCKS, RG_BLOCK)
    h = (_rg_direction(cf, cb, w_ga[0], b_ga[0], w_gx[0], b_gx[0], lam[0], False)
         + _rg_direction(cf, cb, w_ga[1], b_ga[1], w_gx[1], b_gx[1], lam[1], True))
    y = h.astype(x.dtype) * jax.nn.gelu(g)
    return y @ w_out


def _peer(x, w_q, subkeys, u_tab, v_tab):
    bsz, L, D = x.shape
    T = bsz * L
    xt = x.reshape(T, D)
    q = (xt @ w_q).astype(F32).reshape(T, PEER_HEADS, 2, PEER_HALF)
    s = jnp.einsum('thcd,cnd->thcn', q, subkeys.astype(F32))
    sv, si = lax.top_k(s, PEER_TOPK)
    cand_s = (sv[:, :, 0, :, None] + sv[:, :, 1, None, :]).reshape(T, PEER_HEADS, PEER_TOPK * PEER_TOPK)
    cand_e = (si[:, :, 0, :, None] * PEER_NKEYS + si[:, :, 1, None, :]).reshape(T, PEER_HEADS, PEER_TOPK * PEER_TOPK)
    top_s, top_p = lax.top_k(cand_s, PEER_TOPK)
    experts = jnp.take_along_axis(cand_e, top_p, axis=-1).reshape(T, PEER_HEADS * PEER_TOPK)
    gates = jax.nn.softmax(top_s, axis=-1).reshape(T, PEER_HEADS * PEER_TOPK)
    chunk = math.gcd(T, PEER_CHUNK)
    n_blocks = T // chunk

    def expert_block(args):
        xc, ec, gc = args
        act = jax.nn.gelu(jnp.einsum('ckd,cd->ck', u_tab[ec], xc).astype(F32))
        return jnp.einsum('ck,ckd->cd', (gc * act).astype(xc.dtype), v_tab[ec])

    out = lax.map(expert_block, (xt.reshape(n_blocks, chunk, D),
                                 experts.reshape(n_blocks, chunk, PEER_HEADS * PEER_TOPK),
                                 gates.reshape(n_blocks, chunk, PEER_HEADS * PEER_TOPK)))
    return out.reshape(bsz, L, D)


def _trunk(x, p, s5, rg, ln, peer, ple):
    ln1_g, ln1_b, ln2_g, ln2_b = ln
    peer_w_q, peer_subkeys, peer_u, peer_v = peer
    ple_w_proj, ple_w_gate = ple
    for i in range(DEPTH):
        j = i // N_MIXERS
        if i % N_MIXERS == 0:
            mix = _s5_mixer(x, *(w[j] for w in s5))
        else:
            mix = _rg_mixer(x, *(w[j] for w in rg))
        x = _layernorm(ALPHA * x + mix, ln1_g[i], ln1_b[i])
        x = _layernorm(ALPHA * x + _peer(x, peer_w_q[i], peer_subkeys[i], peer_u[i], peer_v[i]), ln2_g[i], ln2_b[i])
        x = x + (p[i] @ ple_w_proj[i]) * jax.nn.sigmoid(x @ ple_w_gate[i])
    return x


def setup_inputs(seed: int = 0) -> dict:
    key = jax.random.key(seed)
    ks = jax.random.split(key, 40)

    def nrm(k, shape, scale):
        return jax.random.normal(k, shape, F32) * scale

    NA, NB, D = N_S5_LAYERS, N_RG_LAYERS, D_MODEL
    G, P, GC = S5_GROUPS, S5_STATE, S5_GROUP
    lam_im_base = math.pi * jnp.arange(P, dtype=F32)
    u_rg = jax.random.uniform(ks[10], (NB, 2, RG_WIDTH), F32, 0.9, 0.999)
    a_rg = u_rg ** (1.0 / RG_C)
    inputs = {
        'x_prompt': nrm(ks[0], (BATCH, SEQ, D), 1.0),
        'x_sample': nrm(ks[1], (DEC_BATCH, DEC_SEQ, D), 1.0),
        'p_prompt': nrm(ks[2], (DEPTH, BATCH, SEQ, PLE_DIM), 1.0),
        'p_sample': nrm(ks[3], (DEPTH, DEC_BATCH, DEC_SEQ, PLE_DIM), 1.0),
        's5_w_in': nrm(ks[4], (NA, D, S5_WIDTH), D ** -0.5),
        's5_lam_re': -0.5 + nrm(ks[5], (NA, 2, G, P), 0.01),
        's5_lam_im': lam_im_base + nrm(ks[6], (NA, 2, G, P), 0.01),
        's5_log_step': jax.random.uniform(ks[7], (NA, 2, G), F32, math.log(S5_DT_MIN), math.log(S5_DT_MAX)),
        's5_b_re': nrm(ks[8], (NA, 2, G, P, GC), (2 * GC) ** -0.5),
        's5_b_im': nrm(ks[9], (NA, 2, G, P, GC), (2 * GC) ** -0.5),
        's5_c_re': nrm(ks[11], (NA, 2, G, GC, P), (2 * P) ** -0.5),
        's5_c_im': nrm(ks[12], (NA, 2, G, GC, P), (2 * P) ** -0.5),
        's5_d': nrm(ks[13], (NA, S5_WIDTH), 1.0),
        's5_w_glu': jnp.concatenate([nrm(ks[14], (NA, S5_WIDTH, D), BETA * S5_WIDTH ** -0.5),
                                     nrm(ks[15], (NA, S5_WIDTH, D), S5_WIDTH ** -0.5)], axis=-1),
        'rg_w_in': nrm(ks[16], (NB, D, 2 * RG_WIDTH), D ** -0.5),
        'rg_conv_w': nrm(ks[17], (NB, RG_CONV, RG_WIDTH), RG_CONV ** -0.5),
        'rg_conv_b': nrm(ks[18], (NB, RG_WIDTH), 0.01),
        'rg_w_gate_a': nrm(ks[19], (NB, 2, RG_BLOCKS, RG_BLOCK, RG_BLOCK), RG_BLOCK ** -0.5),
        'rg_b_gate_a': nrm(ks[20], (NB, 2, RG_WIDTH), 0.01),
        'rg_w_gate_x': nrm(ks[21], (NB, 2, RG_BLOCKS, RG_BLOCK, RG_BLOCK), RG_BLOCK ** -0.5),
        'rg_b_gate_x': nrm(ks[22], (NB, 2, RG_WIDTH), 0.01),
        'rg_lambda': jnp.log(a_rg) - jnp.log1p(-a_rg),
        'rg_w_out': nrm(ks[23], (NB, RG_WIDTH, D), BETA * RG_WIDTH ** -0.5),
        'ln1_g': 1.0 + nrm(ks[24], (DEPTH, D), 0.01),
        'ln1_b': nrm(ks[25], (DEPTH, D), 0.01),
        'ln2_g': 1.0 + nrm(ks[26], (DEPTH, D), 0.01),
        'ln2_b': nrm(ks[27], (DEPTH, D), 0.01),
        'peer_w_q': nrm(ks[28], (DEPTH, D, PEER_HEADS * PEER_QDIM), D ** -0.5),
        'peer_subkeys': nrm(ks[29], (DEPTH, 2, PEER_NKEYS, PEER_HALF), PEER_HALF ** -0.5),
        'peer_u': nrm(ks[30], (DEPTH, PEER_EXPERTS, D), D ** -0.5),
        'peer_v': nrm(ks[31], (DEPTH, PEER_EXPERTS, D), BETA * PEER_HEADS ** -0.5),
        'ple_w_proj': nrm(ks[32], (DEPTH, PLE_DIM, D), PLE_DIM ** -0.5),
        'ple_w_gate': nrm(ks[33], (DEPTH, D, D), D ** -0.5),
    }
    return inputs


def reference(x_prompt, x_sample, p_prompt, p_sample,
              s5_w_in, s5_lam_re, s5_lam_im, s5_log_step, s5_b_re, s5_b_im, s5_c_re, s5_c_im, s5_d, s5_w_glu,
              rg_w_in, rg_conv_w, rg_conv_b, rg_w_gate_a, rg_b_gate_a, rg_w_gate_x, rg_b_gate_x, rg_lambda, rg_w_out,
              ln1_g, ln1_b, ln2_g, ln2_b,
              peer_w_q, peer_subkeys, peer_u, peer_v,
              ple_w_proj, ple_w_gate):
    s5 = (s5_w_in, s5_lam_re, s5_lam_im, s5_log_step, s5_b_re, s5_b_im, s5_c_re, s5_c_im, s5_d, s5_w_glu)
    rg = (rg_w_in, rg_conv_w, rg_conv_b, rg_w_gate_a, rg_b_gate_a, rg_w_gate_x, rg_b_gate_x, rg_lambda, rg_w_out)
    ln = (ln1_g, ln1_b, ln2_g, ln2_b)
    peer = (peer_w_q, peer_subkeys, peer_u, peer_v)
    ple = (ple_w_proj, ple_w_gate)
    y_prompt = _trunk(x_prompt, p_prompt, s5, rg, ln, peer, ple)
    y_sample = _trunk(x_sample, p_sample, s5, rg, ln, peer, ple)
    return (y_prompt, y_sample)
```

```python
import functools
import math

import jax
import jax.numpy as jnp
from jax import lax
from jax.experimental import pallas as pl
from jax.experimental.pallas import tpu as pltpu

F32 = jnp.float32
BF16 = jnp.bfloat16

D_MODEL = 1024
DEPTH = 2
S5_GROUP = 16
S5_GROUPS = 64
S5_STATE = 64
S5_SLABS = 4
S5_SLAB_CH = D_MODEL // S5_SLABS
S5_SLAB_ST = 16 * S5_STATE
RG_BLOCKS = 4
RG_BLOCK = 256
RG_CONV = 4
RG_C = 8.0
PEER_HEADS = 8
PEER_NKEYS = 128
PEER_EXPERTS = PEER_NKEYS * PEER_NKEYS
PEER_HALF = 128
PEER_TOPK = 16
PLE_DIM = 256
ALPHA = (2 * DEPTH) ** 0.25
LN_EPS = 1e-5

LANES = 128
VMEM_LIMIT = 56 * 1024 * 1024
MIX_ROWS = 256
SCAN_COLS = 512

_PAIR_COUNTS = [min(PEER_TOPK, PEER_TOPK // (a + 1)) for a in range(PEER_TOPK)]
_PAIR_OFFS = [sum(_PAIR_COUNTS[:a]) for a in range(PEER_TOPK)]
_N_PAIRS = sum(_PAIR_COUNTS)
_PAIR_ROWS = ((_N_PAIRS + 7) // 8) * 8


def _mm(a, b):
    return jnp.dot(a.astype(BF16), b.astype(BF16), preferred_element_type=F32)


def _layernorm(z, g, b):
    mu = jnp.mean(z, axis=-1, keepdims=True)
    zc = z - mu
    var = jnp.mean(zc * zc, axis=-1, keepdims=True)
    return zc * lax.rsqrt(var + LN_EPS) * g + b


def _cparams(*sem):
    return pltpu.CompilerParams(dimension_semantics=sem, vmem_limit_bytes=VMEM_LIMIT)


def _const_spec(shape):
    nd = len(shape)
    return pl.BlockSpec(shape, lambda *_: (0,) * nd)


def _s5_kernel(*refs, batch, steps, reverse, final):
    if final:
        (x_ref, win_ref, wb_ref, wc_ref, a_ref, yprev_ref, d_ref, wglu_ref, lng_ref, lnb_ref,
         out_ref, bu_scr, h_scr) = refs
    else:
        x_ref, win_ref, wb_ref, wc_ref, a_ref, out_ref, bu_scr, h_scr = refs

    @pl.when(pl.program_id(0) == 0)
    def _():
        h_scr[...] = jnp.zeros_like(h_scr)

    x = x_ref[...]
    u = _mm(x, win_ref[...])
    ub = u.astype(BF16)
    for s in range(S5_SLABS):
        bu_scr[:, s * 2 * S5_SLAB_ST:(s + 1) * 2 * S5_SLAB_ST] = jnp.dot(
            ub[:, s * S5_SLAB_CH:(s + 1) * S5_SLAB_CH], wb_ref[s], preferred_element_type=F32)

    for s in range(S5_SLABS):
        for cb in range(S5_SLAB_ST // SCAN_COLS):
            c_re = s * 2 * S5_SLAB_ST + cb * SCAN_COLS
            c_im = c_re + S5_SLAB_ST
            a_re = jnp.broadcast_to(a_ref[s:s + 1, cb * SCAN_COLS:(cb + 1) * SCAN_COLS], (batch, SCAN_COLS))
            a_im = jnp.broadcast_to(
                a_ref[S5_SLABS + s:S5_SLABS + s + 1, cb * SCAN_COLS:(cb + 1) * SCAN_COLS], (batch, SCAN_COLS))

            def step(k, carry, c_re=c_re, c_im=c_im, a_re=a_re, a_im=a_im):
                h_re, h_im = carry
                t = (steps - 1 - k) if reverse else k
                row = pl.multiple_of(t * batch, batch)
                b_re = bu_scr[pl.ds(row, batch), c_re:c_re + SCAN_COLS]
                b_im = bu_scr[pl.ds(row, batch), c_im:c_im + SCAN_COLS]
                n_re = a_re * h_re - a_im * h_im + b_re
                n_im = a_re * h_im + a_im * h_re + b_im
                bu_scr[pl.ds(row, batch), c_re:c_re + SCAN_COLS] = n_re
                bu_scr[pl.ds(row, batch), c_im:c_im + SCAN_COLS] = n_im
                return n_re, n_im

            h_re, h_im = lax.fori_loop(
                0, steps, step,
                (h_scr[:, c_re:c_re + SCAN_COLS], h_scr[:, c_im:c_im + SCAN_COLS]), unroll=4)
            h_scr[:, c_re:c_re + SCAN_COLS] = h_re
            h_scr[:, c_im:c_im + SCAN_COLS] = h_im

    ys = []
    for s in range(S5_SLABS):
        hb = bu_scr[:, s * 2 * S5_SLAB_ST:(s + 1) * 2 * S5_SLAB_ST].astype(BF16)
        ys.append(jnp.dot(hb, wc_ref[s], preferred_element_type=F32))
    y = jnp.concatenate(ys, axis=1)

    if not final:
        out_ref[...] = y
        return

    y = y + yprev_ref[...] + d_ref[...] * u
    hg = jax.nn.gelu(y)
    vg = _mm(hg, wglu_ref[...])
    mix = vg[:, :D_MODEL] * jax.nn.sigmoid(vg[:, D_MODEL:])
    out_ref[...] = _layernorm(ALPHA * x + mix, lng_ref[...], lnb_ref[...])


def _s5_discretize(lam_re, lam_im, log_step, b_re, b_im):
    step = jnp.exp(log_step.astype(F32))[:, None]
    lr = lam_re.astype(F32)
    li = lam_im.astype(F32)
    mag = jnp.exp(lr * step)
    ang = li * step
    ar = mag * jnp.cos(ang)
    ai = mag * jnp.sin(ang)
    den = lr * lr + li * li
    zr = ar - 1.0
    qr = (zr * lr + ai * li) / den
    qi = (ai * lr - zr * li) / den
    br = b_re.astype(F32)
    bi = b_im.astype(F32)
    bbr = qr[..., None] * br - qi[..., None] * bi
    bbi = qr[..., None] * bi + qi[..., None] * br
    return ar, ai, bbr, bbi


def _s5_weights(lam_re, lam_im, log_step, b_re, b_im, c_re, c_im):
    ar, ai, bbr, bbi = _s5_discretize(lam_re, lam_im, log_step, b_re, b_im)
    eye = jnp.eye(16, dtype=F32)

    def in_blocks(bb):
        bb4 = bb.reshape(S5_SLABS, 16, S5_STATE, S5_GROUP)
        return jnp.einsum('sgpc,gh->sgchp', bb4, eye).reshape(S5_SLABS, S5_SLAB_CH, S5_SLAB_ST)

    def out_blocks(cc):
        cc4 = cc.astype(F32).reshape(S5_SLABS, 16, S5_GROUP, S5_STATE)
        return jnp.einsum('sgcp,gh->sgphc', cc4, eye).reshape(S5_SLABS, S5_SLAB_ST, S5_SLAB_CH)

    wb = jnp.concatenate([in_blocks(bbr), in_blocks(bbi)], axis=2).astype(BF16)
    wc = jnp.concatenate([out_blocks(c_re), -out_blocks(c_im)], axis=1).astype(BF16)
    avec = jnp.concatenate([ar.reshape(S5_SLABS, S5_SLAB_ST), ai.reshape(S5_SLABS, S5_SLAB_ST)], axis=0)
    return wb, wc, avec


def _s5_call(x, win, wb, wc, avec, batch, reverse, tail):
    rows = x.shape[0]
    n = rows // MIX_ROWS
    steps = MIX_ROWS // batch
    final = tail is not None
    if reverse:
        tok = pl.BlockSpec((MIX_ROWS, D_MODEL), lambda i: (n - 1 - i, 0))
    else:
        tok = pl.BlockSpec((MIX_ROWS, D_MODEL), lambda i: (i, 0))
    args = [x, win, wb, wc, avec]
    specs = [tok, _const_spec(win.shape), _const_spec(wb.shape), _const_spec(wc.shape), _const_spec(avec.shape)]
    if final:
        yprev, dskip, wglu, lng, lnb = tail
        args += [yprev, dskip, wglu, lng, lnb]
        specs += [tok, _const_spec(dskip.shape), _const_spec(wglu.shape), _const_spec(lng.shape),
                  _const_spec(lnb.shape)]
    return pl.pallas_call(
        functools.partial(_s5_kernel, batch=batch, steps=steps, reverse=reverse, final=final),
        grid=(n,),
        in_specs=specs,
        out_specs=tok,
        out_shape=jax.ShapeDtypeStruct((rows, D_MODEL), F32),
        scratch_shapes=[pltpu.VMEM((MIX_ROWS, S5_SLABS * 2 * S5_SLAB_ST), F32),
                        pltpu.VMEM((batch, S5_SLABS * 2 * S5_SLAB_ST), F32)],
        compiler_params=_cparams("arbitrary"),
        name="s5_bwd_glu_ln" if final else "s5_fwd",
    )(*args)


def _s5_layer(x, batch, w_in, lam_re, lam_im, log_step, b_re, b_im, c_re, c_im, d_skip, w_glu, ln_g, ln_b):
    win = w_in.astype(BF16)
    wb0, wc0, av0 = _s5_weights(lam_re[0], lam_im[0], log_step[0], b_re[0], b_im[0], c_re[0], c_im[0])
    wb1, wc1, av1 = _s5_weights(lam_re[1], lam_im[1], log_step[1], b_re[1], b_im[1], c_re[1], c_im[1])
    y_fwd = _s5_call(x, win, wb0, wc0, av0, batch, False, None)
    tail = (y_fwd, d_skip.astype(F32)[None, :], w_glu.astype(BF16), ln_g[None, :], ln_b[None, :])
    return _s5_call(x, win, wb1, wc1, av1, batch, True, tail)


def _rg_kernel(*refs, batch, steps, reverse, final, n_chunks):
    if final:
        (x_ref, xp_ref, xn_ref, wr_ref, cw_ref, cb_ref, wa_ref, ba_ref, wx_ref, bx_ref, sp_ref,
         hprev_ref, wg_ref, wo_ref, lng_ref, lnb_ref, out_ref, xe_scr, a_scr, b_scr, h_scr) = refs
    else:
        (x_ref, xp_ref, xn_ref, wr_ref, cw_ref, cb_ref, wa_ref, ba_ref, wx_ref, bx_ref, sp_ref,
         out_ref, xe_scr, a_scr, b_scr, h_scr) = refs
    rows = steps * batch
    i = pl.program_id(0)
    chunk = (n_chunks - 1 - i) if reverse else i

    @pl.when(i == 0)
    def _():
        h_scr[...] = jnp.zeros_like(h_scr)

    x = x_ref[...]
    keep_prev = (chunk > 0).astype(F32)
    keep_next = (chunk < n_chunks - 1).astype(F32)
    xe_scr[0:batch, :] = xp_ref[...] * keep_prev
    xe_scr[batch:batch + rows, :] = x
    xe_scr[batch + rows:, :] = xn_ref[...] * keep_next
    r_ext = _mm(xe_scr[...], wr_ref[...])
    xe_scr[...] = r_ext

    c = cb_ref[...] + xe_scr[0:rows, :] * cw_ref[0:1, :]
    for k in range(1, RG_CONV):
        c = c + xe_scr[k * batch:k * batch + rows, :] * cw_ref[k:k + 1, :]

    cb16 = c.astype(BF16)
    ga = []
    gx = []
    for blk in range(RG_BLOCKS):
        cs = cb16[:, blk * RG_BLOCK:(blk + 1) * RG_BLOCK]
        ga.append(jnp.dot(cs, wa_ref[blk], preferred_element_type=F32))
        gx.append(jnp.dot(cs, wx_ref[blk], preferred_element_type=F32))
    r_gate = jax.nn.sigmoid(jnp.concatenate(ga, axis=1) + ba_ref[...])
    i_gate = jax.nn.sigmoid(jnp.concatenate(gx, axis=1) + bx_ref[...])
    log_a = -RG_C * r_gate * sp_ref[...]
    a_scr[...] = jnp.exp(log_a)
    th = jnp.tanh(log_a)
    b_scr[...] = jnp.sqrt(-2.0 * th / (1.0 - th)) * (i_gate * c)

    def step(k, h):
        t = (steps - 1 - k) if reverse else k
        row = pl.multiple_of(t * batch, batch)
        h = a_scr[pl.ds(row, batch), :] * h + b_scr[pl.ds(row, batch), :]
        b_scr[pl.ds(row, batch), :] = h
        return h

    h_scr[...] = lax.fori_loop(0, steps, step, h_scr[...], unroll=4)

    if not final:
        out_ref[...] = b_scr[...]
        return

    h_tot = b_scr[...] + hprev_ref[...]
    g = _mm(x, wg_ref[...])
    y = h_tot * jax.nn.gelu(g)
    mix = _mm(y, wo_ref[...])
    out_ref[...] = _layernorm(ALPHA * x + mix, lng_ref[...], lnb_ref[...])


def _rg_call(x, wr, cw, cb, wa, ba, wx, bx, sp, batch, reverse, tail):
    rows = x.shape[0]
    n = rows // MIX_ROWS
    steps = MIX_ROWS // batch
    final = tail is not None
    per_prev = MIX_ROWS // batch
    per_next = MIX_ROWS // (2 * batch)
    last_next = rows // (2 * batch) - 1

    def cidx(i):
        return (n - 1 - i) if reverse else i

    tok = pl.BlockSpec((MIX_ROWS, D_MODEL), lambda i: (cidx(i), 0))
    prev = pl.BlockSpec((batch, D_MODEL), lambda i: (jnp.maximum(cidx(i) * per_prev - 1, 0), 0))
    nxt = pl.BlockSpec((2 * batch, D_MODEL), lambda i: (jnp.minimum((cidx(i) + 1) * per_next, last_next), 0))
    args = [x, x, x, wr, cw, cb, wa, ba, wx, bx, sp]
    specs = [tok, prev, nxt] + [_const_spec(a.shape) for a in args[3:]]
    if final:
        args += list(tail)
        specs += [tok] + [_const_spec(a.shape) for a in tail[1:]]
    return pl.pallas_call(
        functools.partial(_rg_kernel, batch=batch, steps=steps, reverse=reverse, final=final, n_chunks=n),
        grid=(n,),
        in_specs=specs,
        out_specs=tok,
        out_shape=jax.ShapeDtypeStruct((rows, D_MODEL), F32),
        scratch_shapes=[pltpu.VMEM((MIX_ROWS + 3 * batch, D_MODEL), F32),
                        pltpu.VMEM((MIX_ROWS, D_MODEL), F32),
                        pltpu.VMEM((MIX_ROWS, D_MODEL), F32),
                        pltpu.VMEM((batch, D_MODEL), F32)],
        compiler_params=_cparams("arbitrary"),
        name="rg_bwd_out_ln" if final else "rg_fwd",
    )(*args)


def _rg_layer(x, batch, w_in, conv_w, conv_b, w_ga, b_ga, w_gx, b_gx, lam, w_out, ln_g, ln_b):
    wg = w_in[:, :D_MODEL].astype(BF16)
    wr = w_in[:, D_MODEL:].astype(BF16)
    cw = jnp.concatenate([conv_w.astype(F32), jnp.zeros((8 - RG_CONV, D_MODEL), F32)], axis=0)
    cb = conv_b.astype(F32)[None, :]
    sp = jax.nn.softplus(-lam.astype(F32))
    common = lambda d: (wr, cw, cb, w_ga[d].astype(BF16), b_ga[d][None, :].astype(F32),
                        w_gx[d].astype(BF16), b_gx[d][None, :].astype(F32), sp[d][None, :])
    h_fwd = _rg_call(x, *common(0), batch, False, None)
    tail = (h_fwd, wg, w_out.astype(BF16), ln_g[None, :], ln_b[None, :])
    return _rg_call(x, *common(1), batch, True, tail)


def _peer_topk_kernel(x_ref, wqt_ref, sk_ref, o_ref, qt_scr, vals_scr, rank_scr, e_scr, cand_scr, sel_scr,
                      *, tq):
    neg_inf = jnp.float32(-jnp.inf)
    n_lt = tq // LANES
    qt_scr[...] = lax.dot_general(wqt_ref[...], x_ref[...].astype(BF16), (((1,), (1,)), ((), ())),
                                  preferred_element_type=F32)
    iota_k = lax.broadcasted_iota(jnp.int32, (PEER_NKEYS, LANES), 0).astype(F32)

    def half_body(hc, _):
        q = qt_scr[pl.ds(pl.multiple_of(hc * PEER_HALF, PEER_HALF), PEER_HALF), :]
        s = jnp.dot(sk_ref[hc % 2], q.astype(BF16), preferred_element_type=F32)
        for lt in range(n_lt):
            lanes = slice(lt * LANES, (lt + 1) * LANES)
            s_l = s[:, lanes]

            work = s_l
            rank = jnp.full((PEER_NKEYS, LANES), float(PEER_NKEYS), F32)
            for r in range(PEER_TOPK):
                m = jnp.max(work, axis=0, keepdims=True)
                pos = jnp.min(jnp.where(work == m, iota_k, float(PEER_NKEYS)), axis=0, keepdims=True)
                hit = iota_k == pos
                vals_scr[hc, r:r + 1, lanes] = m
                work = jnp.where(hit, neg_inf, work)
                rank = jnp.where(hit, float(r), rank)
            top = vals_scr[hc, 0:1, lanes]
            rank_scr[hc, :, lanes] = rank
            e_scr[hc, :, lanes] = jnp.where(rank < PEER_TOPK, jnp.exp(s_l - top), 0.0)
        return 0

    lax.fori_loop(0, 2 * PEER_HEADS, half_body, 0)

    iota_p = lax.broadcasted_iota(jnp.int32, (_PAIR_ROWS, LANES), 0).astype(F32)

    def head_body(h, _):
        for lt in range(n_lt):
            lanes = slice(lt * LANES, (lt + 1) * LANES)
            v0 = vals_scr[2 * h, :, lanes]
            v1 = vals_scr[2 * h + 1, :, lanes]
            cand_scr[...] = jnp.full((_PAIR_ROWS, LANES), neg_inf, F32)
            for a in range(PEER_TOPK):
                cand_scr[_PAIR_OFFS[a]:_PAIR_OFFS[a] + _PAIR_COUNTS[a], :] = (
                    v0[a:a + 1, :] + v1[0:_PAIR_COUNTS[a], :])
            cand = cand_scr[...]

            def extract(r, carry):
                work, sel = carry
                m = jnp.max(work, axis=0, keepdims=True)
                pos = jnp.min(jnp.where(work == m, iota_p, float(_PAIR_ROWS)), axis=0, keepdims=True)
                hit = iota_p == pos
                return jnp.where(hit, neg_inf, work), jnp.where(hit, 1.0, sel)

            _, sel = lax.fori_loop(0, PEER_TOPK, extract, (cand, jnp.zeros((_PAIR_ROWS, LANES), F32)))
            sel_scr[...] = sel
            top = cand_scr[0:1, :]
            z = jnp.sum(jnp.where(sel > 0.0, jnp.exp(cand - top), 0.0), axis=0, keepdims=True)
            inv_z = 1.0 / z
            rank0 = rank_scr[2 * h, :, lanes]
            n_dense = jnp.zeros((PEER_NKEYS, LANES), F32)
            for a in range(PEER_TOPK):
                n_a = jnp.sum(sel_scr[_PAIR_OFFS[a]:_PAIR_OFFS[a] + _PAIR_COUNTS[a], :], axis=0, keepdims=True)
                n_dense = jnp.where(rank0 == float(a), n_a, n_dense)
            o_ref[0, h, :, lanes] = e_scr[2 * h, :, lanes] * inv_z
            o_ref[1, h, :, lanes] = n_dense
            o_ref[2, h, :, lanes] = e_scr[2 * h + 1, :, lanes]
            o_ref[3, h, :, lanes] = rank_scr[2 * h + 1, :, lanes]
        return 0

    lax.fori_loop(0, PEER_HEADS, head_body, 0)


def _peer_topk(x, wqt, sk, tq):
    rows = x.shape[0]
    return pl.pallas_call(
        functools.partial(_peer_topk_kernel, tq=tq),
        grid=(rows // tq,),
        in_specs=[pl.BlockSpec((tq, D_MODEL), lambda i: (i, 0)), _const_spec(wqt.shape), _const_spec(sk.shape)],
        out_specs=pl.BlockSpec((4, PEER_HEADS, PEER_NKEYS, tq), lambda i: (0, 0, 0, i)),
        out_shape=jax.ShapeDtypeStruct((4, PEER_HEADS, PEER_NKEYS, rows), F32),
        scratch_shapes=[pltpu.VMEM((2 * PEER_HEADS * PEER_HALF, tq), F32),
                        pltpu.VMEM((2 * PEER_HEADS, PEER_TOPK, tq), F32),
                        pltpu.VMEM((2 * PEER_HEADS, PEER_NKEYS, tq), F32),
                        pltpu.VMEM((2 * PEER_HEADS, PEER_NKEYS, tq), F32),
                        pltpu.VMEM((_PAIR_ROWS, LANES), F32),
                        pltpu.VMEM((_PAIR_ROWS, LANES), F32)],
        compiler_params=_cparams("arbitrary"),
        name="peer_topk",
    )(x, wqt, sk)


def _peer_dense_kernel(x_ref, g_ref, u_ref, vt_ref, o_ref, xt_scr, acc_scr, act_scr, wt_scr, *, tq, ec):
    e = pl.program_id(1)
    keys_per_chunk = ec // PEER_NKEYS

    @pl.when(e == 0)
    def _():
        xt_scr[...] = x_ref[...].T.astype(BF16)
        acc_scr[...] = jnp.zeros_like(acc_scr)

    act_scr[...] = jax.nn.gelu(jnp.dot(u_ref[...], xt_scr[...], preferred_element_type=F32))

    key0 = pl.multiple_of(e * keys_per_chunk, keys_per_chunk)
    for lt in range(tq // LANES):
        lanes = slice(lt * LANES, (lt + 1) * LANES)
        a8 = [g_ref[0, h, pl.ds(key0, keys_per_chunk), lanes] for h in range(PEER_HEADS)]
        n8 = [g_ref[1, h, pl.ds(key0, keys_per_chunk), lanes] for h in range(PEER_HEADS)]
        for ii in range(keys_per_chunk):
            rows_ = slice(ii * PEER_NKEYS, (ii + 1) * PEER_NKEYS)
            gate = jnp.zeros((PEER_NKEYS, LANES), F32)
            for h in range(PEER_HEADS):
                gate = gate + a8[h][ii:ii + 1, :] * jnp.where(
                    g_ref[3, h, :, lanes] < n8[h][ii:ii + 1, :], g_ref[2, h, :, lanes], 0.0)
            wt_scr[rows_, lanes] = (gate * act_scr[rows_, lanes]).astype(BF16)
    acc_scr[...] += jnp.dot(vt_ref[...], wt_scr[...], preferred_element_type=F32)

    @pl.when(e == pl.num_programs(1) - 1)
    def _():
        o_ref[...] = acc_scr[...].T


def _peer_dense(x, gates, u16, vt16, tq, ec):
    rows = x.shape[0]
    return pl.pallas_call(
        functools.partial(_peer_dense_kernel, tq=tq, ec=ec),
        grid=(rows // tq, PEER_EXPERTS // ec),
        in_specs=[pl.BlockSpec((tq, D_MODEL), lambda t, e: (t, 0)),
                  pl.BlockSpec((4, PEER_HEADS, PEER_NKEYS, tq), lambda t, e: (0, 0, 0, t)),
                  pl.BlockSpec((ec, D_MODEL), lambda t, e: (e, 0)),
                  pl.BlockSpec((D_MODEL, ec), lambda t, e: (0, e))],
        out_specs=pl.BlockSpec((tq, D_MODEL), lambda t, e: (t, 0)),
        out_shape=jax.ShapeDtypeStruct((rows, D_MODEL), F32),
        scratch_shapes=[pltpu.VMEM((D_MODEL, tq), BF16),
                        pltpu.VMEM((D_MODEL, tq), F32),
                        pltpu.VMEM((ec, tq), F32),
                        pltpu.VMEM((ec, tq), BF16)],
        compiler_params=_cparams("arbitrary", "arbitrary"),
        name="peer_dense",
    )(x, gates, u16, vt16)


def _ln_ple_kernel(x_ref, peer_ref, p_ref, lng_ref, lnb_ref, wp_ref, wg_ref, o_ref):
    x2 = _layernorm(ALPHA * x_ref[...] + peer_ref[...], lng_ref[...], lnb_ref[...])
    o_ref[...] = x2 + _mm(p_ref[...], wp_ref[...]) * jax.nn.sigmoid(_mm(x2, wg_ref[...]))


def _ln_ple(x, peer, p, lng, lnb, wp, wg, tq):
    rows = x.shape[0]
    tok = pl.BlockSpec((tq, D_MODEL), lambda i: (i, 0))
    return pl.pallas_call(
        _ln_ple_kernel,
        grid=(rows // tq,),
        in_specs=[tok, tok, pl.BlockSpec((tq, PLE_DIM), lambda i: (i, 0)), _const_spec(lng.shape),
                  _const_spec(lnb.shape), _const_spec(wp.shape), _const_spec(wg.shape)],
        out_specs=tok,
        out_shape=jax.ShapeDtypeStruct((rows, D_MODEL), F32),
        compiler_params=_cparams("arbitrary"),
        name="ln_ple",
    )(x, peer, p, lng, lnb, wp, wg)


def _token_tile(rows, want):
    return math.gcd(rows, want)


def _trunk(x, p, s5, rg, ln, peer, ple):
    batch, seq, _ = x.shape
    rows = batch * seq
    ln1_g, ln1_b, ln2_g, ln2_b = ln
    peer_w_q, peer_subkeys, peer_u, peer_v = peer
    ple_w_proj, ple_w_gate = ple
    xt = jnp.transpose(x, (1, 0, 2)).reshape(rows, D_MODEL)
    pt = jnp.transpose(p, (0, 2, 1, 3)).reshape(DEPTH, rows, PLE_DIM)
    for i in range(DEPTH):
        j = i // 2
        if i % 2 == 0:
            xt = _s5_layer(xt, batch, *(w[j] for w in s5), ln1_g[i], ln1_b[i])
        else:
            xt = _rg_layer(xt, batch, *(w[j] for w in rg), ln1_g[i], ln1_b[i])
        gates = _peer_topk(xt, peer_w_q[i].T.astype(BF16), peer_subkeys[i].astype(BF16), _token_tile(rows, 256))
        peer_out = _peer_dense(xt, gates, peer_u[i].astype(BF16), peer_v[i].T.astype(BF16),
                               _token_tile(rows, 512), 1024)
        xt = _ln_ple(xt, peer_out, pt[i], ln2_g[i][None, :], ln2_b[i][None, :],
                     ple_w_proj[i].astype(BF16), ple_w_gate[i].astype(BF16), _token_tile(rows, 512))
    return jnp.transpose(xt.reshape(seq, batch, D_MODEL), (1, 0, 2))


def kernel(x_prompt, x_sample, p_prompt, p_sample, s5_w_in, s5_lam_re, s5_lam_im, s5_log_step, s5_b_re, s5_b_im, s5_c_re, s5_c_im, s5_d, s5_w_glu, rg_w_in, rg_conv_w, rg_conv_b, rg_w_gate_a, rg_b_gate_a, rg_w_gate_x, rg_b_gate_x, rg_lambda, rg_w_out, ln1_g, ln1_b, ln2_g, ln2_b, peer_w_q, peer_subkeys, peer_u, peer_v, ple_w_proj, ple_w_gate):
    s5 = (s5_w_in, s5_lam_re, s5_lam_im, s5_log_step, s5_b_re, s5_b_im, s5_c_re, s5_c_im, s5_d, s5_w_glu)
    rg = (rg_w_in, rg_conv_w, rg_conv_b, rg_w_gate_a, rg_b_gate_a, rg_w_gate_x, rg_b_gate_x, rg_lambda, rg_w_out)
    ln = (ln1_g, ln1_b, ln2_g, ln2_b)
    peer = (peer_w_q, peer_subkeys, peer_u, peer_v)
    ple = (ple_w_proj, ple_w_gate)
    y_prompt = _trunk(x_prompt, p_prompt, s5, rg, ln, peer, ple)
    y_sample = _trunk(x_sample, p_sample, s5, rg, ln, peer, ple)
    return (y_prompt, y_sample)
```

```python
import functools
import math

import jax
import jax.numpy as jnp
from jax import lax
from jax.experimental import pallas as pl
from jax.experimental.pallas import tpu as pltpu

F32 = jnp.float32
BF16 = jnp.bfloat16

D_MODEL = 1024
DEPTH = 2
S5_GROUP = 16
S5_GROUPS = 64
S5_STATE = 64
S5_SLABS = 4
S5_SLAB_CH = D_MODEL // S5_SLABS
S5_SLAB_ST = 16 * S5_STATE
RG_BLOCKS = 4
RG_BLOCK = 256
RG_CONV = 4
RG_C = 8.0
PEER_HEADS = 8
PEER_NKEYS = 128
PEER_EXPERTS = PEER_NKEYS * PEER_NKEYS
PEER_HALF = 128
PEER_TOPK = 16
PLE_DIM = 256
ALPHA = (2 * DEPTH) ** 0.25
LN_EPS = 1e-5

LANES = 128
VMEM_LIMIT = 56 * 1024 * 1024
MIX_ROWS = 256
SCAN_COLS = 512

_PAIR_COUNTS = [min(PEER_TOPK, PEER_TOPK // (a + 1)) for a in range(PEER_TOPK)]
_PAIR_OFFS = [sum(_PAIR_COUNTS[:a]) for a in range(PEER_TOPK)]
_N_PAIRS = sum(_PAIR_COUNTS)
_PAIR_ROWS = ((_N_PAIRS + 7) // 8) * 8


def _mm(a, b):
    return jnp.dot(a.astype(BF16), b.astype(BF16), preferred_element_type=F32)


def _layernorm(z, g, b):
    mu = jnp.mean(z, axis=-1, keepdims=True)
    zc = z - mu
    var = jnp.mean(zc * zc, axis=-1, keepdims=True)
    return zc * lax.rsqrt(var + LN_EPS) * g + b


def _cparams(*sem):
    return pltpu.CompilerParams(dimension_semantics=sem, vmem_limit_bytes=VMEM_LIMIT)


def _const_spec(shape):
    nd = len(shape)
    return pl.BlockSpec(shape, lambda *_: (0,) * nd)


def _s5_kernel(*refs, batch, steps, reverse, final):
    if final:
        (x_ref, win_ref, wb_ref, wc_ref, a_ref, yprev_ref, d_ref, wglu_ref, lng_ref, lnb_ref,
         out_ref, bu_scr, h_scr) = refs
    else:
        x_ref, win_ref, wb_ref, wc_ref, a_ref, out_ref, bu_scr, h_scr = refs

    @pl.when(pl.program_id(0) == 0)
    def _():
        h_scr[...] = jnp.zeros_like(h_scr)

    x = x_ref[...]
    u = _mm(x, win_ref[...])
    ub = u.astype(BF16)
    for s in range(S5_SLABS):
        bu_scr[:, s * 2 * S5_SLAB_ST:(s + 1) * 2 * S5_SLAB_ST] = jnp.dot(
            ub[:, s * S5_SLAB_CH:(s + 1) * S5_SLAB_CH], wb_ref[s], preferred_element_type=F32)

    for s in range(S5_SLABS):
        for cb in range(S5_SLAB_ST // SCAN_COLS):
            c_re = s * 2 * S5_SLAB_ST + cb * SCAN_COLS
            c_im = c_re + S5_SLAB_ST
            a_re = jnp.broadcast_to(a_ref[s:s + 1, cb * SCAN_COLS:(cb + 1) * SCAN_COLS], (batch, SCAN_COLS))
            a_im = jnp.broadcast_to(
                a_ref[S5_SLABS + s:S5_SLABS + s + 1, cb * SCAN_COLS:(cb + 1) * SCAN_COLS], (batch, SCAN_COLS))

            def step(k, carry, c_re=c_re, c_im=c_im, a_re=a_re, a_im=a_im):
                h_re, h_im = carry
                t = (steps - 1 - k) if reverse else k
                row = pl.multiple_of(t * batch, batch)
                b_re = bu_scr[pl.ds(row, batch), c_re:c_re + SCAN_COLS]
                b_im = bu_scr[pl.ds(row, batch), c_im:c_im + SCAN_COLS]
                n_re = a_re * h_re - a_im * h_im + b_re
                n_im = a_re * h_im + a_im * h_re + b_im
                bu_scr[pl.ds(row, batch), c_re:c_re + SCAN_COLS] = n_re
                bu_scr[pl.ds(row, batch), c_im:c_im + SCAN_COLS] = n_im
                return n_re, n_im

            h_re, h_im = lax.fori_loop(
                0, steps, step,
                (h_scr[:, c_re:c_re + SCAN_COLS], h_scr[:, c_im:c_im + SCAN_COLS]), unroll=4)
            h_scr[:, c_re:c_re + SCAN_COLS] = h_re
            h_scr[:, c_im:c_im + SCAN_COLS] = h_im

    ys = []
    for s in range(S5_SLABS):
        hb = bu_scr[:, s * 2 * S5_SLAB_ST:(s + 1) * 2 * S5_SLAB_ST].astype(BF16)
        ys.append(jnp.dot(hb, wc_ref[s], preferred_element_type=F32))
    y = jnp.concatenate(ys, axis=1)

    if not final:
        out_ref[...] = y
        return

    y = y + yprev_ref[...] + d_ref[...] * u
    hg = jax.nn.gelu(y)
    vg = _mm(hg, wglu_ref[...])
    mix = vg[:, :D_MODEL] * jax.nn.sigmoid(vg[:, D_MODEL:])
    out_ref[...] = _layernorm(ALPHA * x + mix, lng_ref[...], lnb_ref[...])


def _s5_discretize(lam_re, lam_im, log_step, b_re, b_im):
    step = jnp.exp(log_step.astype(F32))[:, None]
    lr = lam_re.astype(F32)
    li = lam_im.astype(F32)
    mag = jnp.exp(lr * step)
    ang = li * step
    ar = mag * jnp.cos(ang)
    ai = mag * jnp.sin(ang)
    den = lr * lr + li * li
    zr = ar - 1.0
    qr = (zr * lr + ai * li) / den
    qi = (ai * lr - zr * li) / den
    br = b_re.astype(F32)
    bi = b_im.astype(F32)
    bbr = qr[..., None] * br - qi[..., None] * bi
    bbi = qr[..., None] * bi + qi[..., None] * br
    return ar, ai, bbr, bbi


def _s5_weights(lam_re, lam_im, log_step, b_re, b_im, c_re, c_im):
    ar, ai, bbr, bbi = _s5_discretize(lam_re, lam_im, log_step, b_re, b_im)
    eye = jnp.eye(16, dtype=F32)

    def in_blocks(bb):
        bb4 = bb.reshape(S5_SLABS, 16, S5_STATE, S5_GROUP)
        return jnp.einsum('sgpc,gh->sgchp', bb4, eye).reshape(S5_SLABS, S5_SLAB_CH, S5_SLAB_ST)

    def out_blocks(cc):
        cc4 = cc.astype(F32).reshape(S5_SLABS, 16, S5_GROUP, S5_STATE)
        return jnp.einsum('sgcp,gh->sgphc', cc4, eye).reshape(S5_SLABS, S5_SLAB_ST, S5_SLAB_CH)

    wb = jnp.concatenate([in_blocks(bbr), in_blocks(bbi)], axis=2).astype(BF16)
    wc = jnp.concatenate([out_blocks(c_re), -out_blocks(c_im)], axis=1).astype(BF16)
    avec = jnp.concatenate([ar.reshape(S5_SLABS, S5_SLAB_ST), ai.reshape(S5_SLABS, S5_SLAB_ST)], axis=0)
    return wb, wc, avec


def _s5_call(x, win, wb, wc, avec, batch, reverse, tail):
    rows = x.shape[0]
    n = rows // MIX_ROWS
    steps = MIX_ROWS // batch
    final = tail is not None
    if reverse:
        tok = pl.BlockSpec((MIX_ROWS, D_MODEL), lambda i: (n - 1 - i, 0))
    else:
        tok = pl.BlockSpec((MIX_ROWS, D_MODEL), lambda i: (i, 0))
    args = [x, win, wb, wc, avec]
    specs = [tok, _const_spec(win.shape), _const_spec(wb.shape), _const_spec(wc.shape), _const_spec(avec.shape)]
    if final:
        yprev, dskip, wglu, lng, lnb = tail
        args += [yprev, dskip, wglu, lng, lnb]
        specs += [tok, _const_spec(dskip.shape), _const_spec(wglu.shape), _const_spec(lng.shape),
                  _const_spec(lnb.shape)]
    return pl.pallas_call(
        functools.partial(_s5_kernel, batch=batch, steps=steps, reverse=reverse, final=final),
        grid=(n,),
        in_specs=specs,
        out_specs=tok,
        out_shape=jax.ShapeDtypeStruct((rows, D_MODEL), F32),
        scratch_shapes=[pltpu.VMEM((MIX_ROWS, S5_SLABS * 2 * S5_SLAB_ST), F32),
                        pltpu.VMEM((batch, S5_SLABS * 2 * S5_SLAB_ST), F32)],
        compiler_params=_cparams("arbitrary"),
        name="s5_bwd_glu_ln" if final else "s5_fwd",
    )(*args)


def _s5_layer(x, batch, w_in, lam_re, lam_im, log_step, b_re, b_im, c_re, c_im, d_skip, w_glu, ln_g, ln_b):
    win = w_in.astype(BF16)
    wb0, wc0, av0 = _s5_weights(lam_re[0], lam_im[0], log_step[0], b_re[0], b_im[0], c_re[0], c_im[0])
    wb1, wc1, av1 = _s5_weights(lam_re[1], lam_im[1], log_step[1], b_re[1], b_im[1], c_re[1], c_im[1])
    y_fwd = _s5_call(x, win, wb0, wc0, av0, batch, False, None)
    tail = (y_fwd, d_skip.astype(F32)[None, :], w_glu.astype(BF16), ln_g[None, :], ln_b[None, :])
    return _s5_call(x, win, wb1, wc1, av1, batch, True, tail)


def _rg_kernel(*refs, batch, steps, reverse, final, n_chunks):
    if final:
        (x_ref, xp_ref, xn_ref, wr_ref, cw_ref, cb_ref, wa_ref, ba_ref, wx_ref, bx_ref, sp_ref,
         hprev_ref, wg_ref, wo_ref, lng_ref, lnb_ref, out_ref, xe_scr, a_scr, b_scr, h_scr) = refs
    else:
        (x_ref, xp_ref, xn_ref, wr_ref, cw_ref, cb_ref, wa_ref, ba_ref, wx_ref, bx_ref, sp_ref,
         out_ref, xe_scr, a_scr, b_scr, h_scr) = refs
    rows = steps * batch
    i = pl.program_id(0)
    chunk = (n_chunks - 1 - i) if reverse else i

    @pl.when(i == 0)
    def _():
        h_scr[...] = jnp.zeros_like(h_scr)

    x = x_ref[...]
    keep_prev = (chunk > 0).astype(F32)
    keep_next = (chunk < n_chunks - 1).astype(F32)
    xe_scr[0:batch, :] = xp_ref[...] * keep_prev
    xe_scr[batch:batch + rows, :] = x
    xe_scr[batch + rows:, :] = xn_ref[...] * keep_next
    r_ext = _mm(xe_scr[...], wr_ref[...])
    xe_scr[...] = r_ext

    c = cb_ref[...] + xe_scr[0:rows, :] * cw_ref[0:1, :]
    for k in range(1, RG_CONV):
        c = c + xe_scr[k * batch:k * batch + rows, :] * cw_ref[k:k + 1, :]

    cb16 = c.astype(BF16)
    ga = []
    gx = []
    for blk in range(RG_BLOCKS):
        cs = cb16[:, blk * RG_BLOCK:(blk + 1) * RG_BLOCK]
        ga.append(jnp.dot(cs, wa_ref[blk], preferred_element_type=F32))
        gx.append(jnp.dot(cs, wx_ref[blk], preferred_element_type=F32))
    r_gate = jax.nn.sigmoid(jnp.concatenate(ga, axis=1) + ba_ref[...])
    i_gate = jax.nn.sigmoid(jnp.concatenate(gx, axis=1) + bx_ref[...])
    log_a = -RG_C * r_gate * sp_ref[...]
    a_scr[...] = jnp.exp(log_a)
    th = jnp.tanh(log_a)
    b_scr[...] = jnp.sqrt(-2.0 * th / (1.0 - th)) * (i_gate * c)

    def step(k, h):
        t = (steps - 1 - k) if reverse else k
        row = pl.multiple_of(t * batch, batch)
        h = a_scr[pl.ds(row, batch), :] * h + b_scr[pl.ds(row, batch), :]
        b_scr[pl.ds(row, batch), :] = h
        return h

    h_scr[...] = lax.fori_loop(0, steps, step, h_scr[...], unroll=4)

    if not final:
        out_ref[...] = b_scr[...]
        return

    h_tot = b_scr[...] + hprev_ref[...]
    g = _mm(x, wg_ref[...])
    y = h_tot * jax.nn.gelu(g)
    mix = _mm(y, wo_ref[...])
    out_ref[...] = _layernorm(ALPHA * x + mix, lng_ref[...], lnb_ref[...])


def _rg_call(x, wr, cw, cb, wa, ba, wx, bx, sp, batch, reverse, tail):
    rows = x.shape[0]
    n = rows // MIX_ROWS
    steps = MIX_ROWS // batch
    final = tail is not None
    per_prev = MIX_ROWS // batch
    per_next = MIX_ROWS // (2 * batch)
    last_next = rows // (2 * batch) - 1

    def cidx(i):
        return (n - 1 - i) if reverse else i

    tok = pl.BlockSpec((MIX_ROWS, D_MODEL), lambda i: (cidx(i), 0))
    prev = pl.BlockSpec((batch, D_MODEL), lambda i: (jnp.maximum(cidx(i) * per_prev - 1, 0), 0))
    nxt = pl.BlockSpec((2 * batch, D_MODEL), lambda i: (jnp.minimum((cidx(i) + 1) * per_next, last_next), 0))
    args = [x, x, x, wr, cw, cb, wa, ba, wx, bx, sp]
    specs = [tok, prev, nxt] + [_const_spec(a.shape) for a in args[3:]]
    if final:
        args += list(tail)
        specs += [tok] + [_const_spec(a.shape) for a in tail[1:]]
    return pl.pallas_call(
        functools.partial(_rg_kernel, batch=batch, steps=steps, reverse=reverse, final=final, n_chunks=n),
        grid=(n,),
        in_specs=specs,
        out_specs=tok,
        out_shape=jax.ShapeDtypeStruct((rows, D_MODEL), F32),
        scratch_shapes=[pltpu.VMEM((MIX_ROWS + 3 * batch, D_MODEL), F32),
                        pltpu.VMEM((MIX_ROWS, D_MODEL), F32),
                        pltpu.VMEM((MIX_ROWS, D_MODEL), F32),
                        pltpu.VMEM((batch, D_MODEL), F32)],
        compiler_params=_cparams("arbitrary"),
        name="rg_bwd_out_ln" if final else "rg_fwd",
    )(*args)


def _rg_layer(x, batch, w_in, conv_w, conv_b, w_ga, b_ga, w_gx, b_gx, lam, w_out, ln_g, ln_b):
    wg = w_in[:, :D_MODEL].astype(BF16)
    wr = w_in[:, D_MODEL:].astype(BF16)
    cw = jnp.concatenate([conv_w.astype(F32), jnp.zeros((8 - RG_CONV, D_MODEL), F32)], axis=0)
    cb = conv_b.astype(F32)[None, :]
    sp = jax.nn.softplus(-lam.astype(F32))
    common = lambda d: (wr, cw, cb, w_ga[d].astype(BF16), b_ga[d][None, :].astype(F32),
                        w_gx[d].astype(BF16), b_gx[d][None, :].astype(F32), sp[d][None, :])
    h_fwd = _rg_call(x, *common(0), batch, False, None)
    tail = (h_fwd, wg, w_out.astype(BF16), ln_g[None, :], ln_b[None, :])
    return _rg_call(x, *common(1), batch, True, tail)


def _sorting_pairs(n):
    pairs = []
    p = 1
    while p < n:
        k = p
        while k >= 1:
            for j in range(k % p, n - k, 2 * k):
                for i in range(min(k, n - j - k)):
                    if (i + j) // (2 * p) == (i + j + k) // (2 * p):
                        pairs.append((i + j, i + j + k))
            k //= 2
        p *= 2
    return pairs


_SORT16 = _sorting_pairs(16)
_SORT8 = _sorting_pairs(8)


def _exchange(x, i, j):
    x[i], x[j] = jnp.maximum(x[i], x[j]), jnp.minimum(x[i], x[j])


def _bitonic_merge(x):
    d = len(x) // 2
    while d >= 1:
        for i in range(len(x)):
            if i & d == 0:
                _exchange(x, i, i + d)
        d //= 2


def _merge_top16_across_sublanes(x, shifts):
    for shift in shifts:
        y = [pltpu.roll(v, shift, 0) for v in x]
        x = [jnp.maximum(x[k], y[PEER_TOPK - 1 - k]) for k in range(PEER_TOPK)]
        _bitonic_merge(x)
    return x


def _count_ge(tiles, thr):
    cnt = None
    for t in tiles:
        c = jnp.where(t >= thr, 1.0, 0.0)
        cnt = c if cnt is None else cnt + c
    for shift in (4, 2, 1):
        cnt = cnt + pltpu.roll(cnt, shift, 0)
    return cnt


def _peer_topk_fast(sk_ref, an_ref, br_ref, qt_scr, tq):
    neg_inf = jnp.float32(-jnp.inf)
    sub_iota = lax.broadcasted_iota(jnp.int32, (8, LANES), 0)
    sub_is = [sub_iota == r for r in range(8)]

    def head_body(h, bad):
        for lt in range(tq // LANES):
            lanes = slice(lt * LANES, (lt + 1) * LANES)
            tiles = []
            tops = []
            for c in range(2):
                row0 = pl.multiple_of((2 * h + c) * PEER_HALF, PEER_HALF)
                q = qt_scr[pl.ds(row0, PEER_HALF), lanes]
                s = jnp.dot(sk_ref[c], q.astype(BF16), preferred_element_type=F32)
                t = [s[v * 8:(v + 1) * 8, :] for v in range(PEER_NKEYS // 8)]
                x = list(t)
                for i, j in _SORT16:
                    _exchange(x, i, j)
                tiles.append(t)
                tops.append(_merge_top16_across_sublanes(x, (4, 2, 1)))
            v0, v1 = tops

            cand = [[v0[a] + v1[b] for b in range(_PAIR_COUNTS[a])] for a in range(PEER_TOPK)]
            flat = [cand[a][b] for a in range(PEER_TOPK) for b in range(_PAIR_COUNTS[a])]
            packed = []
            for v in range(_PAIR_ROWS // 8):
                p = jnp.full((8, LANES), neg_inf, F32)
                for r in range(8):
                    if v * 8 + r < _N_PAIRS:
                        p = jnp.where(sub_is[r], flat[v * 8 + r], p)
                packed.append(p)
            x = packed + [jnp.full((8, LANES), neg_inf, F32)]
            for i, j in _SORT8:
                _exchange(x, i, j)
            y = [pltpu.roll(v, 4, 0) for v in x]
            x = x + y[::-1]
            _bitonic_merge(x)
            x = _merge_top16_across_sublanes(x, (2, 1))
            tau = x[PEER_TOPK - 1]
            bad = bad + jnp.where(_count_ge(packed, tau) != float(PEER_TOPK), 1.0, 0.0)

            top = flat[0]
            z = None
            for p in packed:
                e = jnp.where(p >= tau, jnp.exp(p - top), 0.0)
                z = e if z is None else z + e
            for shift in (4, 2, 1):
                z = z + pltpu.roll(z, shift, 0)
            inv_z = 1.0 / z

            counts = []
            for a in range(PEER_TOPK):
                n_a = None
                for b in range(_PAIR_COUNTS[a]):
                    c = jnp.where(cand[a][b] >= tau, 1.0, 0.0)
                    n_a = c if n_a is None else n_a + c
                counts.append(n_a)

            over0 = _count_ge(tiles[0], v0[PEER_TOPK - 1]) > float(PEER_TOPK)
            over1 = _count_ge(tiles[1], v1[PEER_TOPK - 1]) > float(PEER_TOPK)
            bad = bad + jnp.where(over0, counts[PEER_TOPK - 1], 0.0)
            for a in range(PEER_TOPK - 1):
                bad = bad + jnp.where(v0[a] == v0[a + 1], jnp.abs(counts[a] - counts[a + 1]), 0.0)
            for a in range(PEER_TOPK):
                edge = _PAIR_COUNTS[a]
                full = counts[a] == float(edge)
                if edge < PEER_TOPK:
                    tie = v1[edge - 1] == v1[edge]
                else:
                    tie = over1
                bad = bad + jnp.where(tie, jnp.where(full, 1.0, 0.0), 0.0)

            for v in range(PEER_NKEYS // 8):
                rows_ = slice(v * 8, (v + 1) * 8)
                s0 = tiles[0][v]
                s1 = tiles[1][v]
                n_dense = jnp.zeros((8, LANES), F32)
                rank1 = jnp.zeros((8, LANES), F32)
                for a in range(PEER_TOPK):
                    n_dense = jnp.where(s0 == v0[a], counts[a], n_dense)
                    rank1 = rank1 + jnp.where(v1[a] > s1, 1.0, 0.0)
                an_ref[0, h, rows_, lanes] = jnp.exp(s0 - v0[0]) * inv_z
                an_ref[1, h, rows_, lanes] = n_dense
                br_ref[0, h, rows_, lanes] = jnp.exp(s1 - v1[0]).astype(BF16)
                br_ref[1, h, rows_, lanes] = rank1.astype(BF16)
        return bad

    return lax.fori_loop(0, PEER_HEADS, head_body, jnp.zeros((8, LANES), F32))


def _peer_topk_exact(sk_ref, an_ref, br_ref, qt_scr, vals_scr, rank_scr, e_scr, cand_scr, sel_scr, tq):
    neg_inf = jnp.float32(-jnp.inf)
    n_lt = tq // LANES
    iota_k = lax.broadcasted_iota(jnp.int32, (PEER_NKEYS, LANES), 0).astype(F32)

    def half_body(hc, _):
        q = qt_scr[pl.ds(pl.multiple_of(hc * PEER_HALF, PEER_HALF), PEER_HALF), :]
        s = jnp.dot(sk_ref[hc % 2], q.astype(BF16), preferred_element_type=F32)
        for lt in range(n_lt):
            lanes = slice(lt * LANES, (lt + 1) * LANES)
            s_l = s[:, lanes]

            work = s_l
            rank = jnp.full((PEER_NKEYS, LANES), float(PEER_NKEYS), F32)
            for r in range(PEER_TOPK):
                m = jnp.max(work, axis=0, keepdims=True)
                pos = jnp.min(jnp.where(work == m, iota_k, float(PEER_NKEYS)), axis=0, keepdims=True)
                hit = iota_k == pos
                vals_scr[hc, r:r + 1, lanes] = m
                work = jnp.where(hit, neg_inf, work)
                rank = jnp.where(hit, float(r), rank)
            top = vals_scr[hc, 0:1, lanes]
            rank_scr[hc, :, lanes] = rank
            e_scr[hc, :, lanes] = jnp.where(rank < PEER_TOPK, jnp.exp(s_l - top), 0.0)
        return 0

    lax.fori_loop(0, 2 * PEER_HEADS, half_body, 0)

    iota_p = lax.broadcasted_iota(jnp.int32, (_PAIR_ROWS, LANES), 0).astype(F32)

    def head_body(h, _):
        for lt in range(n_lt):
            lanes = slice(lt * LANES, (lt + 1) * LANES)
            v0 = vals_scr[2 * h, :, lanes]
            v1 = vals_scr[2 * h + 1, :, lanes]
            cand_scr[...] = jnp.full((_PAIR_ROWS, LANES), neg_inf, F32)
            for a in range(PEER_TOPK):
                cand_scr[_PAIR_OFFS[a]:_PAIR_OFFS[a] + _PAIR_COUNTS[a], :] = (
                    v0[a:a + 1, :] + v1[0:_PAIR_COUNTS[a], :])
            cand = cand_scr[...]

            def extract(r, carry):
                work, sel = carry
                m = jnp.max(work, axis=0, keepdims=True)
                pos = jnp.min(jnp.where(work == m, iota_p, float(_PAIR_ROWS)), axis=0, keepdims=True)
                hit = iota_p == pos
                return jnp.where(hit, neg_inf, work), jnp.where(hit, 1.0, sel)

            _, sel = lax.fori_loop(0, PEER_TOPK, extract, (cand, jnp.zeros((_PAIR_ROWS, LANES), F32)))
            sel_scr[...] = sel
            top = cand_scr[0:1, :]
            z = jnp.sum(jnp.where(sel > 0.0, jnp.exp(cand - top), 0.0), axis=0, keepdims=True)
            inv_z = 1.0 / z
            rank0 = rank_scr[2 * h, :, lanes]
            n_dense = jnp.zeros((PEER_NKEYS, LANES), F32)
            for a in range(PEER_TOPK):
                n_a = jnp.sum(sel_scr[_PAIR_OFFS[a]:_PAIR_OFFS[a] + _PAIR_COUNTS[a], :], axis=0, keepdims=True)
                n_dense = jnp.where(rank0 == float(a), n_a, n_dense)
            an_ref[0, h, :, lanes] = e_scr[2 * h, :, lanes] * inv_z
            an_ref[1, h, :, lanes] = n_dense
            br_ref[0, h, :, lanes] = e_scr[2 * h + 1, :, lanes].astype(BF16)
            br_ref[1, h, :, lanes] = rank_scr[2 * h + 1, :, lanes].astype(BF16)
        return 0

    lax.fori_loop(0, PEER_HEADS, head_body, 0)


def _peer_topk_kernel(x_ref, wqt_ref, sk_ref, an_ref, br_ref, qt_scr, vals_scr, rank_scr, e_scr, cand_scr,
                      sel_scr, *, tq):
    qt_scr[...] = lax.dot_general(wqt_ref[...], x_ref[...].astype(BF16), (((1,), (1,)), ((), ())),
                                  preferred_element_type=F32)
    bad = _peer_topk_fast(sk_ref, an_ref, br_ref, qt_scr, tq)

    @pl.when(jnp.max(bad) > 0.0)
    def _():
        _peer_topk_exact(sk_ref, an_ref, br_ref, qt_scr, vals_scr, rank_scr, e_scr, cand_scr, sel_scr, tq)


def _peer_topk(x, wqt, sk, tq):
    rows = x.shape[0]
    return pl.pallas_call(
        functools.partial(_peer_topk_kernel, tq=tq),
        grid=(rows // tq,),
        in_specs=[pl.BlockSpec((tq, D_MODEL), lambda i: (i, 0)), _const_spec(wqt.shape), _const_spec(sk.shape)],
        out_specs=[pl.BlockSpec((2, PEER_HEADS, PEER_NKEYS, tq), lambda i: (0, 0, 0, i)),
                   pl.BlockSpec((2, PEER_HEADS, PEER_NKEYS, tq), lambda i: (0, 0, 0, i))],
        out_shape=[jax.ShapeDtypeStruct((2, PEER_HEADS, PEER_NKEYS, rows), F32),
                   jax.ShapeDtypeStruct((2, PEER_HEADS, PEER_NKEYS, rows), BF16)],
        scratch_shapes=[pltpu.VMEM((2 * PEER_HEADS * PEER_HALF, tq), F32),
                        pltpu.VMEM((2 * PEER_HEADS, PEER_TOPK, tq), F32),
                        pltpu.VMEM((2 * PEER_HEADS, PEER_NKEYS, tq), F32),
                        pltpu.VMEM((2 * PEER_HEADS, PEER_NKEYS, tq), F32),
                        pltpu.VMEM((_PAIR_ROWS, LANES), F32),
                        pltpu.VMEM((_PAIR_ROWS, LANES), F32)],
        compiler_params=_cparams("arbitrary"),
        name="peer_topk",
    )(x, wqt, sk)


PEER_CHUNK = 8 * PEER_NKEYS
PEER_N_CHUNKS = PEER_EXPERTS // PEER_CHUNK


def _peer_dense_kernel(x_ref, an_ref, br_ref, u_ref, vt_ref, o_ref, xt_scr, acc_scr, br_scr, act_a, act_b,
                       wt_a, wt_b, *, tq):
    k = pl.program_id(1)

    @pl.when(k == 0)
    def _():
        xt_scr[...] = x_ref[...].T.astype(BF16)
        acc_scr[...] = jnp.zeros_like(acc_scr)
        br_scr[...] = br_ref[...]
        act_b[...] = jnp.zeros_like(act_b)
        wt_a[...] = jnp.zeros_like(wt_a)

    def up(half, act):
        act[...] = jnp.dot(u_ref[half * PEER_CHUNK:(half + 1) * PEER_CHUNK, :], xt_scr[...],
                           preferred_element_type=F32)

    def gate_chunk(chunk, act, wt):
        key0 = pl.multiple_of(jnp.clip(chunk, 0, PEER_N_CHUNKS - 1) * 8, 8)
        for lt in range(tq // LANES):
            lanes = slice(lt * LANES, (lt + 1) * LANES)
            for ii in range(8):
                rows_ = slice(ii * PEER_NKEYS, (ii + 1) * PEER_NKEYS)
                gate = None
                for h in range(PEER_HEADS):
                    a8 = an_ref[0, h, pl.ds(key0, 8), lanes]
                    n8 = an_ref[1, h, pl.ds(key0, 8), lanes]
                    a = jnp.broadcast_to(a8[ii:ii + 1, :], (PEER_NKEYS, LANES)).astype(BF16)
                    n = jnp.broadcast_to(n8[ii:ii + 1, :], (PEER_NKEYS, LANES)).astype(BF16)
                    term = a * jnp.where(br_scr[1, h, :, lanes] < n, br_scr[0, h, :, lanes], 0)
                    gate = term if gate is None else gate + term
                wt[rows_, lanes] = gate * jax.nn.gelu(act[rows_, lanes]).astype(BF16)

    def down(half, wt):
        acc_scr[...] += jnp.dot(vt_ref[:, half * PEER_CHUNK:(half + 1) * PEER_CHUNK], wt[...],
                                preferred_element_type=F32)

    gate_chunk(2 * k - 1, act_b, wt_b)
    down(0, wt_a)
    up(0, act_a)
    gate_chunk(2 * k, act_a, wt_a)
    down(1, wt_b)
    up(1, act_b)

    @pl.when(k == pl.num_programs(1) - 1)
    def _():
        o_ref[...] = acc_scr[...].T


def _peer_dense(x, an, br, u16, vt16, tq):
    rows = x.shape[0]
    n_steps = PEER_N_CHUNKS // 2 + 1
    last = PEER_N_CHUNKS // 2 - 1
    gate_spec = pl.BlockSpec((2, PEER_HEADS, PEER_NKEYS, tq), lambda t, k: (0, 0, 0, t))
    return pl.pallas_call(
        functools.partial(_peer_dense_kernel, tq=tq),
        grid=(rows // tq, n_steps),
        in_specs=[pl.BlockSpec((tq, D_MODEL), lambda t, k: (t, 0)),
                  gate_spec, gate_spec,
                  pl.BlockSpec((2 * PEER_CHUNK, D_MODEL), lambda t, k: (jnp.minimum(k, last), 0)),
                  pl.BlockSpec((D_MODEL, 2 * PEER_CHUNK), lambda t, k: (0, jnp.maximum(k - 1, 0)))],
        out_specs=pl.BlockSpec((tq, D_MODEL), lambda t, k: (t, 0)),
        out_shape=jax.ShapeDtypeStruct((rows, D_MODEL), F32),
        scratch_shapes=[pltpu.VMEM((D_MODEL, tq), BF16),
                        pltpu.VMEM((D_MODEL, tq), F32),
                        pltpu.VMEM((2, PEER_HEADS, PEER_NKEYS, tq), BF16),
                        pltpu.VMEM((PEER_CHUNK, tq), F32),
                        pltpu.VMEM((PEER_CHUNK, tq), F32),
                        pltpu.VMEM((PEER_CHUNK, tq), BF16),
                        pltpu.VMEM((PEER_CHUNK, tq), BF16)],
        compiler_params=_cparams("arbitrary", "arbitrary"),
        name="peer_dense",
    )(x, an, br, u16, vt16)


def _ln_ple_kernel(x_ref, peer_ref, p_ref, lng_ref, lnb_ref, wp_ref, wg_ref, o_ref):
    x2 = _layernorm(ALPHA * x_ref[...] + peer_ref[...], lng_ref[...], lnb_ref[...])
    o_ref[...] = x2 + _mm(p_ref[...], wp_ref[...]) * jax.nn.sigmoid(_mm(x2, wg_ref[...]))


def _ln_ple(x, peer, p, lng, lnb, wp, wg, tq):
    rows = x.shape[0]
    tok = pl.BlockSpec((tq, D_MODEL), lambda i: (i, 0))
    return pl.pallas_call(
        _ln_ple_kernel,
        grid=(rows // tq,),
        in_specs=[tok, tok, pl.BlockSpec((tq, PLE_DIM), lambda i: (i, 0)), _const_spec(lng.shape),
                  _const_spec(lnb.shape), _const_spec(wp.shape), _const_spec(wg.shape)],
        out_specs=tok,
        out_shape=jax.ShapeDtypeStruct((rows, D_MODEL), F32),
        compiler_params=_cparams("arbitrary"),
        name="ln_ple",
    )(x, peer, p, lng, lnb, wp, wg)


def _token_tile(rows, want):
    return math.gcd(rows, want)


def _trunk(x, p, s5, rg, ln, peer, ple):
    batch, seq, _ = x.shape
    rows = batch * seq
    ln1_g, ln1_b, ln2_g, ln2_b = ln
    peer_w_q, peer_subkeys, peer_u, peer_v = peer
    ple_w_proj, ple_w_gate = ple
    xt = jnp.transpose(x, (1, 0, 2)).reshape(rows, D_MODEL)
    pt = jnp.transpose(p, (0, 2, 1, 3)).reshape(DEPTH, rows, PLE_DIM)
    for i in range(DEPTH):
        j = i // 2
        if i % 2 == 0:
            xt = _s5_layer(xt, batch, *(w[j] for w in s5), ln1_g[i], ln1_b[i])
        else:
            xt = _rg_layer(xt, batch, *(w[j] for w in rg), ln1_g[i], ln1_b[i])
        an, br = _peer_topk(xt, peer_w_q[i].T.astype(BF16), peer_subkeys[i].astype(BF16), _token_tile(rows, 256))
        peer_out = _peer_dense(xt, an, br, peer_u[i].astype(BF16), peer_v[i].T.astype(BF16),
                               _token_tile(rows, 512))
        xt = _ln_ple(xt, peer_out, pt[i], ln2_g[i][None, :], ln2_b[i][None, :],
                     ple_w_proj[i].astype(BF16), ple_w_gate[i].astype(BF16), _token_tile(rows, 512))
    return jnp.transpose(xt.reshape(seq, batch, D_MODEL), (1, 0, 2))


def kernel(x_prompt, x_sample, p_prompt, p_sample, s5_w_in, s5_lam_re, s5_lam_im, s5_log_step, s5_b_re, s5_b_im, s5_c_re, s5_c_im, s5_d, s5_w_glu, rg_w_in, rg_conv_w, rg_conv_b, rg_w_gate_a, rg_b_gate_a, rg_w_gate_x, rg_b_gate_x, rg_lambda, rg_w_out, ln1_g, ln1_b, ln2_g, ln2_b, peer_w_q, peer_subkeys, peer_u, peer_v, ple_w_proj, ple_w_gate):
    s5 = (s5_w_in, s5_lam_re, s5_lam_im, s5_log_step, s5_b_re, s5_b_im, s5_c_re, s5_c_im, s5_d, s5_w_glu)
    rg = (rg_w_in, rg_conv_w, rg_conv_b, rg_w_gate_a, rg_b_gate_a, rg_w_gate_x, rg_b_gate_x, rg_lambda, rg_w_out)
    ln = (ln1_g, ln1_b, ln2_g, ln2_b)
    peer = (peer_w_q, peer_subkeys, peer_u, peer_v)
    ple = (ple_w_proj, ple_w_gate)
    y_prompt = _trunk(x_prompt, p_prompt, s5, rg, ln, peer, ple)
    y_sample = _trunk(x_sample, p_sample, s5, rg, ln, peer, ple)
    return (y_prompt, y_sample)
```

```python
import functools
import math

import jax
import jax.numpy as jnp
from jax import lax
from jax.experimental import pallas as pl
from jax.experimental.pallas import tpu as pltpu

F32 = jnp.float32
BF16 = jnp.bfloat16

D_MODEL = 1024
DEPTH = 2
S5_GROUP = 16
S5_GROUPS = 64
S5_STATE = 64
S5_SLABS = 4
S5_SLAB_CH = D_MODEL // S5_SLABS
S5_SLAB_ST = 16 * S5_STATE
RG_BLOCKS = 4
RG_BLOCK = 256
RG_CONV = 4
RG_C = 8.0
PEER_HEADS = 8
PEER_NKEYS = 128
PEER_EXPERTS = PEER_NKEYS * PEER_NKEYS
PEER_HALF = 128
PEER_TOPK = 16
PLE_DIM = 256
ALPHA = (2 * DEPTH) ** 0.25
LN_EPS = 1e-5

LANES = 128
VMEM_LIMIT = 56 * 1024 * 1024
MIX_ROWS = 256
SCAN_COLS = 512

_PAIR_COUNTS = [min(PEER_TOPK, PEER_TOPK // (a + 1)) for a in range(PEER_TOPK)]
_PAIR_OFFS = [sum(_PAIR_COUNTS[:a]) for a in range(PEER_TOPK)]
_N_PAIRS = sum(_PAIR_COUNTS)
_PAIR_ROWS = ((_N_PAIRS + 7) // 8) * 8


def _mm(a, b):
    return jnp.dot(a.astype(BF16), b.astype(BF16), preferred_element_type=F32)


def _layernorm(z, g, b):
    mu = jnp.mean(z, axis=-1, keepdims=True)
    zc = z - mu
    var = jnp.mean(zc * zc, axis=-1, keepdims=True)
    return zc * lax.rsqrt(var + LN_EPS) * g + b


def _cparams(*sem):
    return pltpu.CompilerParams(dimension_semantics=sem, vmem_limit_bytes=VMEM_LIMIT)


def _const_spec(shape):
    nd = len(shape)
    return pl.BlockSpec(shape, lambda *_: (0,) * nd)


def _s5_kernel(*refs, batch, steps, reverse, final):
    if final:
        (x_ref, win_ref, wb_ref, wc_ref, a_ref, yprev_ref, d_ref, wglu_ref, lng_ref, lnb_ref,
         out_ref, bu_scr, h_scr) = refs
    else:
        x_ref, win_ref, wb_ref, wc_ref, a_ref, out_ref, bu_scr, h_scr = refs

    @pl.when(pl.program_id(0) == 0)
    def _():
        h_scr[...] = jnp.zeros_like(h_scr)

    x = x_ref[...]
    u = _mm(x, win_ref[...])
    ub = u.astype(BF16)
    for s in range(S5_SLABS):
        bu_scr[:, s * 2 * S5_SLAB_ST:(s + 1) * 2 * S5_SLAB_ST] = jnp.dot(
            ub[:, s * S5_SLAB_CH:(s + 1) * S5_SLAB_CH], wb_ref[s], preferred_element_type=F32)

    for s in range(S5_SLABS):
        for cb in range(S5_SLAB_ST // SCAN_COLS):
            c_re = s * 2 * S5_SLAB_ST + cb * SCAN_COLS
            c_im = c_re + S5_SLAB_ST
            a_re = jnp.broadcast_to(a_ref[s:s + 1, cb * SCAN_COLS:(cb + 1) * SCAN_COLS], (batch, SCAN_COLS))
            a_im = jnp.broadcast_to(
                a_ref[S5_SLABS + s:S5_SLABS + s + 1, cb * SCAN_COLS:(cb + 1) * SCAN_COLS], (batch, SCAN_COLS))

            def step(k, carry, c_re=c_re, c_im=c_im, a_re=a_re, a_im=a_im):
                h_re, h_im = carry
                t = (steps - 1 - k) if reverse else k
                row = pl.multiple_of(t * batch, batch)
                b_re = bu_scr[pl.ds(row, batch), c_re:c_re + SCAN_COLS]
                b_im = bu_scr[pl.ds(row, batch), c_im:c_im + SCAN_COLS]
                n_re = a_re * h_re - a_im * h_im + b_re
                n_im = a_re * h_im + a_im * h_re + b_im
                bu_scr[pl.ds(row, batch), c_re:c_re + SCAN_COLS] = n_re
                bu_scr[pl.ds(row, batch), c_im:c_im + SCAN_COLS] = n_im
                return n_re, n_im

            h_re, h_im = lax.fori_loop(
                0, steps, step,
                (h_scr[:, c_re:c_re + SCAN_COLS], h_scr[:, c_im:c_im + SCAN_COLS]), unroll=4)
            h_scr[:, c_re:c_re + SCAN_COLS] = h_re
            h_scr[:, c_im:c_im + SCAN_COLS] = h_im

    ys = []
    for s in range(S5_SLABS):
        hb = bu_scr[:, s * 2 * S5_SLAB_ST:(s + 1) * 2 * S5_SLAB_ST].astype(BF16)
        ys.append(jnp.dot(hb, wc_ref[s], preferred_element_type=F32))
    y = jnp.concatenate(ys, axis=1)

    if not final:
        out_ref[...] = y
        return

    y = y + yprev_ref[...] + d_ref[...] * u
    hg = jax.nn.gelu(y)
    vg = _mm(hg, wglu_ref[...])
    mix = vg[:, :D_MODEL] * jax.nn.sigmoid(vg[:, D_MODEL:])
    out_ref[...] = _layernorm(ALPHA * x + mix, lng_ref[...], lnb_ref[...])


def _s5_discretize(lam_re, lam_im, log_step, b_re, b_im):
    step = jnp.exp(log_step.astype(F32))[:, None]
    lr = lam_re.astype(F32)
    li = lam_im.astype(F32)
    mag = jnp.exp(lr * step)
    ang = li * step
    ar = mag * jnp.cos(ang)
    ai = mag * jnp.sin(ang)
    den = lr * lr + li * li
    zr = ar - 1.0
    qr = (zr * lr + ai * li) / den
    qi = (ai * lr - zr * li) / den
    br = b_re.astype(F32)
    bi = b_im.astype(F32)
    bbr = qr[..., None] * br - qi[..., None] * bi
    bbi = qr[..., None] * bi + qi[..., None] * br
    return ar, ai, bbr, bbi


def _s5_weights(lam_re, lam_im, log_step, b_re, b_im, c_re, c_im):
    ar, ai, bbr, bbi = _s5_discretize(lam_re, lam_im, log_step, b_re, b_im)
    eye = jnp.eye(16, dtype=F32)

    def in_blocks(bb):
        bb4 = bb.reshape(S5_SLABS, 16, S5_STATE, S5_GROUP)
        return jnp.einsum('sgpc,gh->sgchp', bb4, eye).reshape(S5_SLABS, S5_SLAB_CH, S5_SLAB_ST)

    def out_blocks(cc):
        cc4 = cc.astype(F32).reshape(S5_SLABS, 16, S5_GROUP, S5_STATE)
        return jnp.einsum('sgcp,gh->sgphc', cc4, eye).reshape(S5_SLABS, S5_SLAB_ST, S5_SLAB_CH)

    wb = jnp.concatenate([in_blocks(bbr), in_blocks(bbi)], axis=2).astype(BF16)
    wc = jnp.concatenate([out_blocks(c_re), -out_blocks(c_im)], axis=1).astype(BF16)
    avec = jnp.concatenate([ar.reshape(S5_SLABS, S5_SLAB_ST), ai.reshape(S5_SLABS, S5_SLAB_ST)], axis=0)
    return wb, wc, avec


def _s5_call(x, win, wb, wc, avec, batch, reverse, tail):
    rows = x.shape[0]
    n = rows // MIX_ROWS
    steps = MIX_ROWS // batch
    final = tail is not None
    if reverse:
        tok = pl.BlockSpec((MIX_ROWS, D_MODEL), lambda i: (n - 1 - i, 0))
    else:
        tok = pl.BlockSpec((MIX_ROWS, D_MODEL), lambda i: (i, 0))
    args = [x, win, wb, wc, avec]
    specs = [tok, _const_spec(win.shape), _const_spec(wb.shape), _const_spec(wc.shape), _const_spec(avec.shape)]
    if final:
        yprev, dskip, wglu, lng, lnb = tail
        args += [yprev, dskip, wglu, lng, lnb]
        specs += [tok, _const_spec(dskip.shape), _const_spec(wglu.shape), _const_spec(lng.shape),
                  _const_spec(lnb.shape)]
    return pl.pallas_call(
        functools.partial(_s5_kernel, batch=batch, steps=steps, reverse=reverse, final=final),
        grid=(n,),
        in_specs=specs,
        out_specs=tok,
        out_shape=jax.ShapeDtypeStruct((rows, D_MODEL), F32),
        scratch_shapes=[pltpu.VMEM((MIX_ROWS, S5_SLABS * 2 * S5_SLAB_ST), F32),
                        pltpu.VMEM((batch, S5_SLABS * 2 * S5_SLAB_ST), F32)],
        compiler_params=_cparams("arbitrary"),
        name="s5_bwd_glu_ln" if final else "s5_fwd",
    )(*args)


def _s5_layer(x, batch, w_in, lam_re, lam_im, log_step, b_re, b_im, c_re, c_im, d_skip, w_glu, ln_g, ln_b):
    win = w_in.astype(BF16)
    wb0, wc0, av0 = _s5_weights(lam_re[0], lam_im[0], log_step[0], b_re[0], b_im[0], c_re[0], c_im[0])
    wb1, wc1, av1 = _s5_weights(lam_re[1], lam_im[1], log_step[1], b_re[1], b_im[1], c_re[1], c_im[1])
    y_fwd = _s5_call(x, win, wb0, wc0, av0, batch, False, None)
    tail = (y_fwd, d_skip.astype(F32)[None, :], w_glu.astype(BF16), ln_g[None, :], ln_b[None, :])
    return _s5_call(x, win, wb1, wc1, av1, batch, True, tail)


def _rg_kernel(*refs, batch, steps, reverse, final, n_chunks):
    if final:
        (x_ref, xp_ref, xn_ref, wr_ref, cw_ref, cb_ref, wa_ref, ba_ref, wx_ref, bx_ref, sp_ref,
         hprev_ref, wg_ref, wo_ref, lng_ref, lnb_ref, out_ref, xe_scr, a_scr, b_scr, h_scr) = refs
    else:
        (x_ref, xp_ref, xn_ref, wr_ref, cw_ref, cb_ref, wa_ref, ba_ref, wx_ref, bx_ref, sp_ref,
         out_ref, xe_scr, a_scr, b_scr, h_scr) = refs
    rows = steps * batch
    i = pl.program_id(0)
    chunk = (n_chunks - 1 - i) if reverse else i

    @pl.when(i == 0)
    def _():
        h_scr[...] = jnp.zeros_like(h_scr)

    x = x_ref[...]
    keep_prev = (chunk > 0).astype(F32)
    keep_next = (chunk < n_chunks - 1).astype(F32)
    xe_scr[0:batch, :] = xp_ref[...] * keep_prev
    xe_scr[batch:batch + rows, :] = x
    xe_scr[batch + rows:, :] = xn_ref[...] * keep_next
    r_ext = _mm(xe_scr[...], wr_ref[...])
    xe_scr[...] = r_ext

    c = cb_ref[...] + xe_scr[0:rows, :] * cw_ref[0:1, :]
    for k in range(1, RG_CONV):
        c = c + xe_scr[k * batch:k * batch + rows, :] * cw_ref[k:k + 1, :]

    cb16 = c.astype(BF16)
    ga = []
    gx = []
    for blk in range(RG_BLOCKS):
        cs = cb16[:, blk * RG_BLOCK:(blk + 1) * RG_BLOCK]
        ga.append(jnp.dot(cs, wa_ref[blk], preferred_element_type=F32))
        gx.append(jnp.dot(cs, wx_ref[blk], preferred_element_type=F32))
    r_gate = jax.nn.sigmoid(jnp.concatenate(ga, axis=1) + ba_ref[...])
    i_gate = jax.nn.sigmoid(jnp.concatenate(gx, axis=1) + bx_ref[...])
    log_a = -RG_C * r_gate * sp_ref[...]
    a_scr[...] = jnp.exp(log_a)
    th = jnp.tanh(log_a)
    b_scr[...] = jnp.sqrt(-2.0 * th / (1.0 - th)) * (i_gate * c)

    def step(k, h):
        t = (steps - 1 - k) if reverse else k
        row = pl.multiple_of(t * batch, batch)
        h = a_scr[pl.ds(row, batch), :] * h + b_scr[pl.ds(row, batch), :]
        b_scr[pl.ds(row, batch), :] = h
        return h

    h_scr[...] = lax.fori_loop(0, steps, step, h_scr[...], unroll=4)

    if not final:
        out_ref[...] = b_scr[...]
        return

    h_tot = b_scr[...] + hprev_ref[...]
    g = _mm(x, wg_ref[...])
    y = h_tot * jax.nn.gelu(g)
    mix = _mm(y, wo_ref[...])
    out_ref[...] = _layernorm(ALPHA * x + mix, lng_ref[...], lnb_ref[...])


def _rg_call(x, wr, cw, cb, wa, ba, wx, bx, sp, batch, reverse, tail):
    rows = x.shape[0]
    n = rows // MIX_ROWS
    steps = MIX_ROWS // batch
    final = tail is not None
    per_prev = MIX_ROWS // batch
    per_next = MIX_ROWS // (2 * batch)
    last_next = rows // (2 * batch) - 1

    def cidx(i):
        return (n - 1 - i) if reverse else i

    tok = pl.BlockSpec((MIX_ROWS, D_MODEL), lambda i: (cidx(i), 0))
    prev = pl.BlockSpec((batch, D_MODEL), lambda i: (jnp.maximum(cidx(i) * per_prev - 1, 0), 0))
    nxt = pl.BlockSpec((2 * batch, D_MODEL), lambda i: (jnp.minimum((cidx(i) + 1) * per_next, last_next), 0))
    args = [x, x, x, wr, cw, cb, wa, ba, wx, bx, sp]
    specs = [tok, prev, nxt] + [_const_spec(a.shape) for a in args[3:]]
    if final:
        args += list(tail)
        specs += [tok] + [_const_spec(a.shape) for a in tail[1:]]
    return pl.pallas_call(
        functools.partial(_rg_kernel, batch=batch, steps=steps, reverse=reverse, final=final, n_chunks=n),
        grid=(n,),
        in_specs=specs,
        out_specs=tok,
        out_shape=jax.ShapeDtypeStruct((rows, D_MODEL), F32),
        scratch_shapes=[pltpu.VMEM((MIX_ROWS + 3 * batch, D_MODEL), F32),
                        pltpu.VMEM((MIX_ROWS, D_MODEL), F32),
                        pltpu.VMEM((MIX_ROWS, D_MODEL), F32),
                        pltpu.VMEM((batch, D_MODEL), F32)],
        compiler_params=_cparams("arbitrary"),
        name="rg_bwd_out_ln" if final else "rg_fwd",
    )(*args)


def _rg_layer(x, batch, w_in, conv_w, conv_b, w_ga, b_ga, w_gx, b_gx, lam, w_out, ln_g, ln_b):
    wg = w_in[:, :D_MODEL].astype(BF16)
    wr = w_in[:, D_MODEL:].astype(BF16)
    cw = jnp.concatenate([conv_w.astype(F32), jnp.zeros((8 - RG_CONV, D_MODEL), F32)], axis=0)
    cb = conv_b.astype(F32)[None, :]
    sp = jax.nn.softplus(-lam.astype(F32))
    common = lambda d: (wr, cw, cb, w_ga[d].astype(BF16), b_ga[d][None, :].astype(F32),
                        w_gx[d].astype(BF16), b_gx[d][None, :].astype(F32), sp[d][None, :])
    h_fwd = _rg_call(x, *common(0), batch, False, None)
    tail = (h_fwd, wg, w_out.astype(BF16), ln_g[None, :], ln_b[None, :])
    return _rg_call(x, *common(1), batch, True, tail)


def _sorting_pairs(n):
    pairs = []
    p = 1
    while p < n:
        k = p
        while k >= 1:
            for j in range(k % p, n - k, 2 * k):
                for i in range(min(k, n - j - k)):
                    if (i + j) // (2 * p) == (i + j + k) // (2 * p):
                        pairs.append((i + j, i + j + k))
            k //= 2
        p *= 2
    return pairs


_SORT16 = _sorting_pairs(16)
_SORT8 = _sorting_pairs(8)


def _exchange(x, i, j):
    x[i], x[j] = jnp.maximum(x[i], x[j]), jnp.minimum(x[i], x[j])


def _bitonic_merge(x):
    d = len(x) // 2
    while d >= 1:
        for i in range(len(x)):
            if i & d == 0:
                _exchange(x, i, i + d)
        d //= 2


def _merge_top16_across_sublanes(x, shifts):
    for shift in shifts:
        y = [pltpu.roll(v, shift, 0) for v in x]
        x = [jnp.maximum(x[k], y[PEER_TOPK - 1 - k]) for k in range(PEER_TOPK)]
        _bitonic_merge(x)
    return x


def _count_ge(tiles, thr):
    cnt = None
    for t in tiles:
        c = jnp.where(t >= thr, 1.0, 0.0)
        cnt = c if cnt is None else cnt + c
    for shift in (4, 2, 1):
        cnt = cnt + pltpu.roll(cnt, shift, 0)
    return cnt


def _peer_topk_fast(sk_ref, an_ref, br_ref, qt_scr, tq):
    neg_inf = jnp.float32(-jnp.inf)
    sub_iota = lax.broadcasted_iota(jnp.int32, (8, LANES), 0)
    sub_is = [sub_iota == r for r in range(8)]

    def head_body(h, bad):
        for lt in range(tq // LANES):
            lanes = slice(lt * LANES, (lt + 1) * LANES)
            tiles = []
            tops = []
            for c in range(2):
                row0 = pl.multiple_of((2 * h + c) * PEER_HALF, PEER_HALF)
                q = qt_scr[pl.ds(row0, PEER_HALF), lanes]
                s = jnp.dot(sk_ref[c], q.astype(BF16), preferred_element_type=F32)
                t = [s[v * 8:(v + 1) * 8, :] for v in range(PEER_NKEYS // 8)]
                x = list(t)
                for i, j in _SORT16:
                    _exchange(x, i, j)
                tiles.append(t)
                tops.append(_merge_top16_across_sublanes(x, (4, 2, 1)))
            v0, v1 = tops

            cand = [[v0[a] + v1[b] for b in range(_PAIR_COUNTS[a])] for a in range(PEER_TOPK)]
            flat = [cand[a][b] for a in range(PEER_TOPK) for b in range(_PAIR_COUNTS[a])]
            packed = []
            for v in range(_PAIR_ROWS // 8):
                p = jnp.full((8, LANES), neg_inf, F32)
                for r in range(8):
                    if v * 8 + r < _N_PAIRS:
                        p = jnp.where(sub_is[r], flat[v * 8 + r], p)
                packed.append(p)
            x = packed + [jnp.full((8, LANES), neg_inf, F32)]
            for i, j in _SORT8:
                _exchange(x, i, j)
            y = [pltpu.roll(v, 4, 0) for v in x]
            x = x + y[::-1]
            _bitonic_merge(x)
            x = _merge_top16_across_sublanes(x, (2, 1))
            tau = x[PEER_TOPK - 1]
            bad = bad + jnp.where(_count_ge(packed, tau) != float(PEER_TOPK), 1.0, 0.0)

            top = flat[0]
            z = None
            for p in packed:
                e = jnp.where(p >= tau, jnp.exp(p - top), 0.0)
                z = e if z is None else z + e
            for shift in (4, 2, 1):
                z = z + pltpu.roll(z, shift, 0)
            inv_z = 1.0 / z

            counts = []
            for a in range(PEER_TOPK):
                n_a = None
                for b in range(_PAIR_COUNTS[a]):
                    c = jnp.where(cand[a][b] >= tau, 1.0, 0.0)
                    n_a = c if n_a is None else n_a + c
                counts.append(n_a)

            over0 = _count_ge(tiles[0], v0[PEER_TOPK - 1]) > float(PEER_TOPK)
            over1 = _count_ge(tiles[1], v1[PEER_TOPK - 1]) > float(PEER_TOPK)
            bad = bad + jnp.where(over0, counts[PEER_TOPK - 1], 0.0)
            for a in range(PEER_TOPK - 1):
                bad = bad + jnp.where(v0[a] == v0[a + 1], jnp.abs(counts[a] - counts[a + 1]), 0.0)
            for a in range(PEER_TOPK):
                edge = _PAIR_COUNTS[a]
                full = counts[a] == float(edge)
                if edge < PEER_TOPK:
                    tie = v1[edge - 1] == v1[edge]
                else:
                    tie = over1
                bad = bad + jnp.where(tie, jnp.where(full, 1.0, 0.0), 0.0)

            for v in range(PEER_NKEYS // 8):
                rows_ = slice(v * 8, (v + 1) * 8)
                s0 = tiles[0][v]
                s1 = tiles[1][v]
                n_dense = jnp.zeros((8, LANES), F32)
                rank1 = jnp.zeros((8, LANES), F32)
                for a in range(PEER_TOPK):
                    n_dense = jnp.where(s0 == v0[a], counts[a], n_dense)
                    rank1 = rank1 + jnp.where(v1[a] > s1, 1.0, 0.0)
                an_ref[0, h, rows_, lanes] = jnp.exp(s0 - v0[0]) * inv_z
                an_ref[1, h, rows_, lanes] = n_dense
                br_ref[0, h, rows_, lanes] = jnp.exp(s1 - v1[0]).astype(BF16)
                br_ref[1, h, rows_, lanes] = rank1.astype(BF16)
        return bad

    return lax.fori_loop(0, PEER_HEADS, head_body, jnp.zeros((8, LANES), F32))


def _peer_topk_exact(sk_ref, an_ref, br_ref, qt_scr, vals_scr, rank_scr, e_scr, cand_scr, sel_scr, tq):
    neg_inf = jnp.float32(-jnp.inf)
    n_lt = tq // LANES
    iota_k = lax.broadcasted_iota(jnp.int32, (PEER_NKEYS, LANES), 0).astype(F32)

    def half_body(hc, _):
        q = qt_scr[pl.ds(pl.multiple_of(hc * PEER_HALF, PEER_HALF), PEER_HALF), :]
        s = jnp.dot(sk_ref[hc % 2], q.astype(BF16), preferred_element_type=F32)
        for lt in range(n_lt):
            lanes = slice(lt * LANES, (lt + 1) * LANES)
            s_l = s[:, lanes]

            work = s_l
            rank = jnp.full((PEER_NKEYS, LANES), float(PEER_NKEYS), F32)
            for r in range(PEER_TOPK):
                m = jnp.max(work, axis=0, keepdims=True)
                pos = jnp.min(jnp.where(work == m, iota_k, float(PEER_NKEYS)), axis=0, keepdims=True)
                hit = iota_k == pos
                vals_scr[hc, r:r + 1, lanes] = m
                work = jnp.where(hit, neg_inf, work)
                rank = jnp.where(hit, float(r), rank)
            top = vals_scr[hc, 0:1, lanes]
            rank_scr[hc, :, lanes] = rank
            e_scr[hc, :, lanes] = jnp.where(rank < PEER_TOPK, jnp.exp(s_l - top), 0.0)
        return 0

    lax.fori_loop(0, 2 * PEER_HEADS, half_body, 0)

    iota_p = lax.broadcasted_iota(jnp.int32, (_PAIR_ROWS, LANES), 0).astype(F32)

    def head_body(h, _):
        for lt in range(n_lt):
            lanes = slice(lt * LANES, (lt + 1) * LANES)
            v0 = vals_scr[2 * h, :, lanes]
            v1 = vals_scr[2 * h + 1, :, lanes]
            cand_scr[...] = jnp.full((_PAIR_ROWS, LANES), neg_inf, F32)
            for a in range(PEER_TOPK):
                cand_scr[_PAIR_OFFS[a]:_PAIR_OFFS[a] + _PAIR_COUNTS[a], :] = (
                    v0[a:a + 1, :] + v1[0:_PAIR_COUNTS[a], :])
            cand = cand_scr[...]

            def extract(r, carry):
                work, sel = carry
                m = jnp.max(work, axis=0, keepdims=True)
                pos = jnp.min(jnp.where(work == m, iota_p, float(_PAIR_ROWS)), axis=0, keepdims=True)
                hit = iota_p == pos
                return jnp.where(hit, neg_inf, work), jnp.where(hit, 1.0, sel)

            _, sel = lax.fori_loop(0, PEER_TOPK, extract, (cand, jnp.zeros((_PAIR_ROWS, LANES), F32)))
            sel_scr[...] = sel
            top = cand_scr[0:1, :]
            z = jnp.sum(jnp.where(sel > 0.0, jnp.exp(cand - top), 0.0), axis=0, keepdims=True)
            inv_z = 1.0 / z
            rank0 = rank_scr[2 * h, :, lanes]
            n_dense = jnp.zeros((PEER_NKEYS, LANES), F32)
            for a in range(PEER_TOPK):
                n_a = jnp.sum(sel_scr[_PAIR_OFFS[a]:_PAIR_OFFS[a] + _PAIR_COUNTS[a], :], axis=0, keepdims=True)
                n_dense = jnp.where(rank0 == float(a), n_a, n_dense)
            an_ref[0, h, :, lanes] = e_scr[2 * h, :, lanes] * inv_z
            an_ref[1, h, :, lanes] = n_dense
            br_ref[0, h, :, lanes] = e_scr[2 * h + 1, :, lanes].astype(BF16)
            br_ref[1, h, :, lanes] = rank_scr[2 * h + 1, :, lanes].astype(BF16)
        return 0

    lax.fori_loop(0, PEER_HEADS, head_body, 0)


def _peer_topk_kernel(x_ref, wqt_ref, sk_ref, an_ref, br_ref, qt_scr, vals_scr, rank_scr, e_scr, cand_scr,
                      sel_scr, *, tq):
    qt_scr[...] = lax.dot_general(wqt_ref[...], x_ref[...].astype(BF16), (((1,), (1,)), ((), ())),
                                  preferred_element_type=F32)
    bad = _peer_topk_fast(sk_ref, an_ref, br_ref, qt_scr, tq)

    @pl.when(jnp.max(bad) > 0.0)
    def _():
        _peer_topk_exact(sk_ref, an_ref, br_ref, qt_scr, vals_scr, rank_scr, e_scr, cand_scr, sel_scr, tq)


def _peer_topk(x, wqt, sk, tq):
    rows = x.shape[0]
    return pl.pallas_call(
        functools.partial(_peer_topk_kernel, tq=tq),
        grid=(rows // tq,),
        in_specs=[pl.BlockSpec((tq, D_MODEL), lambda i: (i, 0)), _const_spec(wqt.shape), _const_spec(sk.shape)],
        out_specs=[pl.BlockSpec((2, PEER_HEADS, PEER_NKEYS, tq), lambda i: (0, 0, 0, i)),
                   pl.BlockSpec((2, PEER_HEADS, PEER_NKEYS, tq), lambda i: (0, 0, 0, i))],
        out_shape=[jax.ShapeDtypeStruct((2, PEER_HEADS, PEER_NKEYS, rows), F32),
                   jax.ShapeDtypeStruct((2, PEER_HEADS, PEER_NKEYS, rows), BF16)],
        scratch_shapes=[pltpu.VMEM((2 * PEER_HEADS * PEER_HALF, tq), F32),
                        pltpu.VMEM((2 * PEER_HEADS, PEER_TOPK, tq), F32),
                        pltpu.VMEM((2 * PEER_HEADS, PEER_NKEYS, tq), F32),
                        pltpu.VMEM((2 * PEER_HEADS, PEER_NKEYS, tq), F32),
                        pltpu.VMEM((_PAIR_ROWS, LANES), F32),
                        pltpu.VMEM((_PAIR_ROWS, LANES), F32)],
        compiler_params=_cparams("arbitrary"),
        name="peer_topk",
    )(x, wqt, sk)


PEER_CHUNK = 8 * PEER_NKEYS
PEER_N_CHUNKS = PEER_EXPERTS // PEER_CHUNK


def _peer_dense_kernel(x_ref, an_ref, br_ref, u_ref, vt_ref, o_ref, xt_scr, acc_scr, br_scr, act_scr, wt_scr,
                       *, tq):
    e = pl.program_id(1)

    @pl.when(e == 0)
    def _():
        xt_scr[...] = x_ref[...].T.astype(BF16)
        acc_scr[...] = jnp.zeros_like(acc_scr)
        br_scr[...] = br_ref[...]

    act_scr[...] = jnp.dot(u_ref[...], xt_scr[...], preferred_element_type=F32)

    key0 = pl.multiple_of(e * 8, 8)
    for lt in range(tq // LANES):
        lanes = slice(lt * LANES, (lt + 1) * LANES)
        for ii in range(8):
            rows_ = slice(ii * PEER_NKEYS, (ii + 1) * PEER_NKEYS)
            gate = None
            for h in range(PEER_HEADS):
                a8 = an_ref[0, h, pl.ds(key0, 8), lanes]
                n8 = an_ref[1, h, pl.ds(key0, 8), lanes]
                a = jnp.broadcast_to(a8[ii:ii + 1, :], (PEER_NKEYS, LANES)).astype(BF16)
                n = jnp.broadcast_to(n8[ii:ii + 1, :], (PEER_NKEYS, LANES)).astype(BF16)
                term = a * jnp.where(br_scr[1, h, :, lanes] < n, br_scr[0, h, :, lanes], 0)
                gate = term if gate is None else gate + term
            wt_scr[rows_, lanes] = gate * jax.nn.gelu(act_scr[rows_, lanes]).astype(BF16)

    acc_scr[...] += jnp.dot(vt_ref[...], wt_scr[...], preferred_element_type=F32)

    @pl.when(e == pl.num_programs(1) - 1)
    def _():
        o_ref[...] = acc_scr[...].T


def _peer_dense(x, an, br, u16, vt16, tq):
    rows = x.shape[0]
    gate_spec = pl.BlockSpec((2, PEER_HEADS, PEER_NKEYS, tq), lambda t, e: (0, 0, 0, t))
    return pl.pallas_call(
        functools.partial(_peer_dense_kernel, tq=tq),
        grid=(rows // tq, PEER_N_CHUNKS),
        in_specs=[pl.BlockSpec((tq, D_MODEL), lambda t, e: (t, 0)),
                  gate_spec, gate_spec,
                  pl.BlockSpec((PEER_CHUNK, D_MODEL), lambda t, e: (e, 0)),
                  pl.BlockSpec((D_MODEL, PEER_CHUNK), lambda t, e: (0, e))],
        out_specs=pl.BlockSpec((tq, D_MODEL), lambda t, e: (t, 0)),
        out_shape=jax.ShapeDtypeStruct((rows, D_MODEL), F32),
        scratch_shapes=[pltpu.VMEM((D_MODEL, tq), BF16),
                        pltpu.VMEM((D_MODEL, tq), F32),
                        pltpu.VMEM((2, PEER_HEADS, PEER_NKEYS, tq), BF16),
                        pltpu.VMEM((PEER_CHUNK, tq), F32),
                        pltpu.VMEM((PEER_CHUNK, tq), BF16)],
        compiler_params=_cparams("arbitrary", "arbitrary"),
        name="peer_dense",
    )(x, an, br, u16, vt16)


def _ln_ple_kernel(x_ref, peer_ref, p_ref, lng_ref, lnb_ref, wp_ref, wg_ref, o_ref):
    x2 = _layernorm(ALPHA * x_ref[...] + peer_ref[...], lng_ref[...], lnb_ref[...])
    o_ref[...] = x2 + _mm(p_ref[...], wp_ref[...]) * jax.nn.sigmoid(_mm(x2, wg_ref[...]))


def _ln_ple(x, peer, p, lng, lnb, wp, wg, tq):
    rows = x.shape[0]
    tok = pl.BlockSpec((tq, D_MODEL), lambda i: (i, 0))
    return pl.pallas_call(
        _ln_ple_kernel,
        grid=(rows // tq,),
        in_specs=[tok, tok, pl.BlockSpec((tq, PLE_DIM), lambda i: (i, 0)), _const_spec(lng.shape),
                  _const_spec(lnb.shape), _const_spec(wp.shape), _const_spec(wg.shape)],
        out_specs=tok,
        out_shape=jax.ShapeDtypeStruct((rows, D_MODEL), F32),
        compiler_params=_cparams("arbitrary"),
        name="ln_ple",
    )(x, peer, p, lng, lnb, wp, wg)


def _token_tile(rows, want):
    return math.gcd(rows, want)


def _trunk(x, p, s5, rg, ln, peer, ple):
    batch, seq, _ = x.shape
    rows = batch * seq
    ln1_g, ln1_b, ln2_g, ln2_b = ln
    peer_w_q, peer_subkeys, peer_u, peer_v = peer
    ple_w_proj, ple_w_gate = ple
    xt = jnp.transpose(x, (1, 0, 2)).reshape(rows, D_MODEL)
    pt = jnp.transpose(p, (0, 2, 1, 3)).reshape(DEPTH, rows, PLE_DIM)
    for i in range(DEPTH):
        j = i // 2
        if i % 2 == 0:
            xt = _s5_layer(xt, batch, *(w[j] for w in s5), ln1_g[i], ln1_b[i])
        else:
            xt = _rg_layer(xt, batch, *(w[j] for w in rg), ln1_g[i], ln1_b[i])
        an, br = _peer_topk(xt, peer_w_q[i].T.astype(BF16), peer_subkeys[i].astype(BF16), _token_tile(rows, 256))
        peer_out = _peer_dense(xt, an, br, peer_u[i].astype(BF16), peer_v[i].T.astype(BF16),
                               _token_tile(rows, 512))
        xt = _ln_ple(xt, peer_out, pt[i], ln2_g[i][None, :], ln2_b[i][None, :],
                     ple_w_proj[i].astype(BF16), ple_w_gate[i].astype(BF16), _token_tile(rows, 512))
    return jnp.transpose(xt.reshape(seq, batch, D_MODEL), (1, 0, 2))


def kernel(x_prompt, x_sample, p_prompt, p_sample, s5_w_in, s5_lam_re, s5_lam_im, s5_log_step, s5_b_re, s5_b_im, s5_c_re, s5_c_im, s5_d, s5_w_glu, rg_w_in, rg_conv_w, rg_conv_b, rg_w_gate_a, rg_b_gate_a, rg_w_gate_x, rg_b_gate_x, rg_lambda, rg_w_out, ln1_g, ln1_b, ln2_g, ln2_b, peer_w_q, peer_subkeys, peer_u, peer_v, ple_w_proj, ple_w_gate):
    s5 = (s5_w_in, s5_lam_re, s5_lam_im, s5_log_step, s5_b_re, s5_b_im, s5_c_re, s5_c_im, s5_d, s5_w_glu)
    rg = (rg_w_in, rg_conv_w, rg_conv_b, rg_w_gate_a, rg_b_gate_a, rg_w_gate_x, rg_b_gate_x, rg_lambda, rg_w_out)
    ln = (ln1_g, ln1_b, ln2_g, ln2_b)
    peer = (peer_w_q, peer_subkeys, peer_u, peer_v)
    ple = (ple_w_proj, ple_w_gate)
    y_prompt = _trunk(x_prompt, p_prompt, s5, rg, ln, peer, ple)
    y_sample = _trunk(x_sample, p_sample, s5, rg, ln, peer, ple)
    return (y_prompt, y_sample)
```

```python
import functools
import math

import jax
import jax.numpy as jnp
from jax import lax
from jax.experimental import pallas as pl
from jax.experimental.pallas import tpu as pltpu

F32 = jnp.float32
BF16 = jnp.bfloat16

D_MODEL = 1024
DEPTH = 2
S5_GROUP = 16
S5_GROUPS = 64
S5_STATE = 64
S5_SLABS = 4
S5_SLAB_CH = D_MODEL // S5_SLABS
S5_SLAB_ST = 16 * S5_STATE
RG_BLOCKS = 4
RG_BLOCK = 256
RG_CONV = 4
RG_C = 8.0
PEER_HEADS = 8
PEER_NKEYS = 128
PEER_EXPERTS = PEER_NKEYS * PEER_NKEYS
PEER_HALF = 128
PEER_TOPK = 16
PLE_DIM = 256
ALPHA = (2 * DEPTH) ** 0.25
LN_EPS = 1e-5

LANES = 128
VMEM_LIMIT = 56 * 1024 * 1024
MIX_ROWS = 512
SCAN_COLS = 512

_PAIR_COUNTS = [min(PEER_TOPK, PEER_TOPK // (a + 1)) for a in range(PEER_TOPK)]
_PAIR_OFFS = [sum(_PAIR_COUNTS[:a]) for a in range(PEER_TOPK)]
_N_PAIRS = sum(_PAIR_COUNTS)
_PAIR_ROWS = ((_N_PAIRS + 7) // 8) * 8


def _mm(a, b):
    return jnp.dot(a.astype(BF16), b.astype(BF16), preferred_element_type=F32)


def _layernorm(z, g, b):
    mu = jnp.mean(z, axis=-1, keepdims=True)
    zc = z - mu
    var = jnp.mean(zc * zc, axis=-1, keepdims=True)
    return zc * lax.rsqrt(var + LN_EPS) * g + b


def _cparams(*sem):
    return pltpu.CompilerParams(dimension_semantics=sem, vmem_limit_bytes=VMEM_LIMIT)


def _const_spec(shape):
    nd = len(shape)
    return pl.BlockSpec(shape, lambda *_: (0,) * nd)


def _s5_kernel(*refs, batch, steps, reverse, final):
    if final:
        (x_ref, win_ref, wb_ref, wc_ref, a_ref, yprev_ref, d_ref, wglu_ref, lng_ref, lnb_ref,
         out_ref, bu_scr, h_scr) = refs
    else:
        x_ref, win_ref, wb_ref, wc_ref, a_ref, out_ref, bu_scr, h_scr = refs

    @pl.when(pl.program_id(0) == 0)
    def _():
        h_scr[...] = jnp.zeros_like(h_scr)

    x = x_ref[...]
    u = _mm(x, win_ref[...])
    ub = u.astype(BF16)
    for s in range(S5_SLABS):
        bu_scr[:, s * 2 * S5_SLAB_ST:(s + 1) * 2 * S5_SLAB_ST] = jnp.dot(
            ub[:, s * S5_SLAB_CH:(s + 1) * S5_SLAB_CH], wb_ref[s], preferred_element_type=F32)

    for s in range(S5_SLABS):
        for cb in range(S5_SLAB_ST // SCAN_COLS):
            c_re = s * 2 * S5_SLAB_ST + cb * SCAN_COLS
            c_im = c_re + S5_SLAB_ST
            a_re = jnp.broadcast_to(a_ref[s:s + 1, cb * SCAN_COLS:(cb + 1) * SCAN_COLS], (batch, SCAN_COLS))
            a_im = jnp.broadcast_to(
                a_ref[S5_SLABS + s:S5_SLABS + s + 1, cb * SCAN_COLS:(cb + 1) * SCAN_COLS], (batch, SCAN_COLS))

            def step(k, carry, c_re=c_re, c_im=c_im, a_re=a_re, a_im=a_im):
                h_re, h_im = carry
                t = (steps - 1 - k) if reverse else k
                row = pl.multiple_of(t * batch, batch)
                b_re = bu_scr[pl.ds(row, batch), c_re:c_re + SCAN_COLS]
                b_im = bu_scr[pl.ds(row, batch), c_im:c_im + SCAN_COLS]
                n_re = a_re * h_re - a_im * h_im + b_re
                n_im = a_re * h_im + a_im * h_re + b_im
                bu_scr[pl.ds(row, batch), c_re:c_re + SCAN_COLS] = n_re
                bu_scr[pl.ds(row, batch), c_im:c_im + SCAN_COLS] = n_im
                return n_re, n_im

            h_re, h_im = lax.fori_loop(
                0, steps, step,
                (h_scr[:, c_re:c_re + SCAN_COLS], h_scr[:, c_im:c_im + SCAN_COLS]), unroll=4)
            h_scr[:, c_re:c_re + SCAN_COLS] = h_re
            h_scr[:, c_im:c_im + SCAN_COLS] = h_im

    ys = []
    for s in range(S5_SLABS):
        hb = bu_scr[:, s * 2 * S5_SLAB_ST:(s + 1) * 2 * S5_SLAB_ST].astype(BF16)
        ys.append(jnp.dot(hb, wc_ref[s], preferred_element_type=F32))
    y = jnp.concatenate(ys, axis=1)

    if not final:
        out_ref[...] = y
        return

    y = y + yprev_ref[...] + d_ref[...] * u
    hg = jax.nn.gelu(y)
    vg = _mm(hg, wglu_ref[...])
    mix = vg[:, :D_MODEL] * jax.nn.sigmoid(vg[:, D_MODEL:])
    out_ref[...] = _layernorm(ALPHA * x + mix, lng_ref[...], lnb_ref[...])


def _s5_discretize(lam_re, lam_im, log_step, b_re, b_im):
    step = jnp.exp(log_step.astype(F32))[:, None]
    lr = lam_re.astype(F32)
    li = lam_im.astype(F32)
    mag = jnp.exp(lr * step)
    ang = li * step
    ar = mag * jnp.cos(ang)
    ai = mag * jnp.sin(ang)
    den = lr * lr + li * li
    zr = ar - 1.0
    qr = (zr * lr + ai * li) / den
    qi = (ai * lr - zr * li) / den
    br = b_re.astype(F32)
    bi = b_im.astype(F32)
    bbr = qr[..., None] * br - qi[..., None] * bi
    bbi = qr[..., None] * bi + qi[..., None] * br
    return ar, ai, bbr, bbi


def _s5_weights(lam_re, lam_im, log_step, b_re, b_im, c_re, c_im):
    ar, ai, bbr, bbi = _s5_discretize(lam_re, lam_im, log_step, b_re, b_im)
    eye = jnp.eye(16, dtype=F32)

    def in_blocks(bb):
        bb4 = bb.reshape(S5_SLABS, 16, S5_STATE, S5_GROUP)
        return jnp.einsum('sgpc,gh->sgchp', bb4, eye).reshape(S5_SLABS, S5_SLAB_CH, S5_SLAB_ST)

    def out_blocks(cc):
        cc4 = cc.astype(F32).reshape(S5_SLABS, 16, S5_GROUP, S5_STATE)
        return jnp.einsum('sgcp,gh->sgphc', cc4, eye).reshape(S5_SLABS, S5_SLAB_ST, S5_SLAB_CH)

    wb = jnp.concatenate([in_blocks(bbr), in_blocks(bbi)], axis=2).astype(BF16)
    wc = jnp.concatenate([out_blocks(c_re), -out_blocks(c_im)], axis=1).astype(BF16)
    avec = jnp.concatenate([ar.reshape(S5_SLABS, S5_SLAB_ST), ai.reshape(S5_SLABS, S5_SLAB_ST)], axis=0)
    return wb, wc, avec


def _s5_call(x, win, wb, wc, avec, batch, reverse, tail):
    rows = x.shape[0]
    n = rows // MIX_ROWS
    steps = MIX_ROWS // batch
    final = tail is not None
    if reverse:
        tok = pl.BlockSpec((MIX_ROWS, D_MODEL), lambda i: (n - 1 - i, 0))
    else:
        tok = pl.BlockSpec((MIX_ROWS, D_MODEL), lambda i: (i, 0))
    args = [x, win, wb, wc, avec]
    specs = [tok, _const_spec(win.shape), _const_spec(wb.shape), _const_spec(wc.shape), _const_spec(avec.shape)]
    if final:
        yprev, dskip, wglu, lng, lnb = tail
        args += [yprev, dskip, wglu, lng, lnb]
        specs += [tok, _const_spec(dskip.shape), _const_spec(wglu.shape), _const_spec(lng.shape),
                  _const_spec(lnb.shape)]
    return pl.pallas_call(
        functools.partial(_s5_kernel, batch=batch, steps=steps, reverse=reverse, final=final),
        grid=(n,),
        in_specs=specs,
        out_specs=tok,
        out_shape=jax.ShapeDtypeStruct((rows, D_MODEL), F32),
        scratch_shapes=[pltpu.VMEM((MIX_ROWS, S5_SLABS * 2 * S5_SLAB_ST), F32),
                        pltpu.VMEM((batch, S5_SLABS * 2 * S5_SLAB_ST), F32)],
        compiler_params=_cparams("arbitrary"),
        name="s5_bwd_glu_ln" if final else "s5_fwd",
    )(*args)


def _s5_layer(x, batch, w_in, lam_re, lam_im, log_step, b_re, b_im, c_re, c_im, d_skip, w_glu, ln_g, ln_b):
    win = w_in.astype(BF16)
    wb0, wc0, av0 = _s5_weights(lam_re[0], lam_im[0], log_step[0], b_re[0], b_im[0], c_re[0], c_im[0])
    wb1, wc1, av1 = _s5_weights(lam_re[1], lam_im[1], log_step[1], b_re[1], b_im[1], c_re[1], c_im[1])
    y_fwd = _s5_call(x, win, wb0, wc0, av0, batch, False, None)
    tail = (y_fwd, d_skip.astype(F32)[None, :], w_glu.astype(BF16), ln_g[None, :], ln_b[None, :])
    return _s5_call(x, win, wb1, wc1, av1, batch, True, tail)


def _rg_kernel(*refs, batch, steps, reverse, final, n_chunks):
    if final:
        (x_ref, xp_ref, xn_ref, wr_ref, cw_ref, cb_ref, wa_ref, ba_ref, wx_ref, bx_ref, sp_ref,
         hprev_ref, wg_ref, wo_ref, lng_ref, lnb_ref, out_ref, xe_scr, a_scr, b_scr, h_scr) = refs
    else:
        (x_ref, xp_ref, xn_ref, wr_ref, cw_ref, cb_ref, wa_ref, ba_ref, wx_ref, bx_ref, sp_ref,
         out_ref, xe_scr, a_scr, b_scr, h_scr) = refs
    rows = steps * batch
    i = pl.program_id(0)
    chunk = (n_chunks - 1 - i) if reverse else i

    @pl.when(i == 0)
    def _():
        h_scr[...] = jnp.zeros_like(h_scr)

    x = x_ref[...]
    keep_prev = (chunk > 0).astype(F32)
    keep_next = (chunk < n_chunks - 1).astype(F32)
    xe_scr[0:batch, :] = xp_ref[...] * keep_prev
    xe_scr[batch:batch + rows, :] = x
    xe_scr[batch + rows:, :] = xn_ref[...] * keep_next
    r_ext = _mm(xe_scr[...], wr_ref[...])
    xe_scr[...] = r_ext

    c = cb_ref[...] + xe_scr[0:rows, :] * cw_ref[0:1, :]
    for k in range(1, RG_CONV):
        c = c + xe_scr[k * batch:k * batch + rows, :] * cw_ref[k:k + 1, :]

    cb16 = c.astype(BF16)
    ga = []
    gx = []
    for blk in range(RG_BLOCKS):
        cs = cb16[:, blk * RG_BLOCK:(blk + 1) * RG_BLOCK]
        ga.append(jnp.dot(cs, wa_ref[blk], preferred_element_type=F32))
        gx.append(jnp.dot(cs, wx_ref[blk], preferred_element_type=F32))
    r_gate = jax.nn.sigmoid(jnp.concatenate(ga, axis=1) + ba_ref[...])
    i_gate = jax.nn.sigmoid(jnp.concatenate(gx, axis=1) + bx_ref[...])
    log_a = -RG_C * r_gate * sp_ref[...]
    a_scr[...] = jnp.exp(log_a)
    th = jnp.tanh(log_a)
    b_scr[...] = jnp.sqrt(-2.0 * th / (1.0 - th)) * (i_gate * c)

    def step(k, h):
        t = (steps - 1 - k) if reverse else k
        row = pl.multiple_of(t * batch, batch)
        h = a_scr[pl.ds(row, batch), :] * h + b_scr[pl.ds(row, batch), :]
        b_scr[pl.ds(row, batch), :] = h
        return h

    h_scr[...] = lax.fori_loop(0, steps, step, h_scr[...], unroll=4)

    if not final:
        out_ref[...] = b_scr[...]
        return

    h_tot = b_scr[...] + hprev_ref[...]
    g = _mm(x, wg_ref[...])
    y = h_tot * jax.nn.gelu(g)
    mix = _mm(y, wo_ref[...])
    out_ref[...] = _layernorm(ALPHA * x + mix, lng_ref[...], lnb_ref[...])


def _rg_call(x, wr, cw, cb, wa, ba, wx, bx, sp, batch, reverse, tail):
    rows = x.shape[0]
    n = rows // MIX_ROWS
    steps = MIX_ROWS // batch
    final = tail is not None
    per_prev = MIX_ROWS // batch
    per_next = MIX_ROWS // (2 * batch)
    last_next = rows // (2 * batch) - 1

    def cidx(i):
        return (n - 1 - i) if reverse else i

    tok = pl.BlockSpec((MIX_ROWS, D_MODEL), lambda i: (cidx(i), 0))
    prev = pl.BlockSpec((batch, D_MODEL), lambda i: (jnp.maximum(cidx(i) * per_prev - 1, 0), 0))
    nxt = pl.BlockSpec((2 * batch, D_MODEL), lambda i: (jnp.minimum((cidx(i) + 1) * per_next, last_next), 0))
    args = [x, x, x, wr, cw, cb, wa, ba, wx, bx, sp]
    specs = [tok, prev, nxt] + [_const_spec(a.shape) for a in args[3:]]
    if final:
        args += list(tail)
        specs += [tok] + [_const_spec(a.shape) for a in tail[1:]]
    return pl.pallas_call(
        functools.partial(_rg_kernel, batch=batch, steps=steps, reverse=reverse, final=final, n_chunks=n),
        grid=(n,),
        in_specs=specs,
        out_specs=tok,
        out_shape=jax.ShapeDtypeStruct((rows, D_MODEL), F32),
        scratch_shapes=[pltpu.VMEM((MIX_ROWS + 3 * batch, D_MODEL), F32),
                        pltpu.VMEM((MIX_ROWS, D_MODEL), F32),
                        pltpu.VMEM((MIX_ROWS, D_MODEL), F32),
                        pltpu.VMEM((batch, D_MODEL), F32)],
        compiler_params=_cparams("arbitrary"),
        name="rg_bwd_out_ln" if final else "rg_fwd",
    )(*args)


def _rg_layer(x, batch, w_in, conv_w, conv_b, w_ga, b_ga, w_gx, b_gx, lam, w_out, ln_g, ln_b):
    wg = w_in[:, :D_MODEL].astype(BF16)
    wr = w_in[:, D_MODEL:].astype(BF16)
    cw = jnp.concatenate([conv_w.astype(F32), jnp.zeros((8 - RG_CONV, D_MODEL), F32)], axis=0)
    cb = conv_b.astype(F32)[None, :]
    sp = jax.nn.softplus(-lam.astype(F32))
    common = lambda d: (wr, cw, cb, w_ga[d].astype(BF16), b_ga[d][None, :].astype(F32),
                        w_gx[d].astype(BF16), b_gx[d][None, :].astype(F32), sp[d][None, :])
    h_fwd = _rg_call(x, *common(0), batch, False, None)
    tail = (h_fwd, wg, w_out.astype(BF16), ln_g[None, :], ln_b[None, :])
    return _rg_call(x, *common(1), batch, True, tail)


def _sorting_pairs(n):
    pairs = []
    p = 1
    while p < n:
        k = p
        while k >= 1:
            for j in range(k % p, n - k, 2 * k):
                for i in range(min(k, n - j - k)):
                    if (i + j) // (2 * p) == (i + j + k) // (2 * p):
                        pairs.append((i + j, i + j + k))
            k //= 2
        p *= 2
    return pairs


_SORT16 = _sorting_pairs(16)
_SORT8 = _sorting_pairs(8)


def _exchange(x, i, j):
    x[i], x[j] = jnp.maximum(x[i], x[j]), jnp.minimum(x[i], x[j])


def _bitonic_merge(x):
    d = len(x) // 2
    while d >= 1:
        for i in range(len(x)):
            if i & d == 0:
                _exchange(x, i, i + d)
        d //= 2


def _merge_top16_across_sublanes(x, shifts):
    for shift in shifts:
        y = [pltpu.roll(v, shift, 0) for v in x]
        x = [jnp.maximum(x[k], y[PEER_TOPK - 1 - k]) for k in range(PEER_TOPK)]
        _bitonic_merge(x)
    return x


def _count_ge(tiles, thr):
    cnt = None
    for t in tiles:
        c = jnp.where(t >= thr, 1.0, 0.0)
        cnt = c if cnt is None else cnt + c
    for shift in (4, 2, 1):
        cnt = cnt + pltpu.roll(cnt, shift, 0)
    return cnt


def _peer_topk_fast(sk_ref, an_ref, br_ref, qt_scr, tq):
    neg_inf = jnp.float32(-jnp.inf)
    sub_iota = lax.broadcasted_iota(jnp.int32, (8, LANES), 0)
    sub_is = [sub_iota == r for r in range(8)]

    def head_body(h, bad):
        for lt in range(tq // LANES):
            lanes = slice(lt * LANES, (lt + 1) * LANES)
            tiles = []
            tops = []
            for c in range(2):
                row0 = pl.multiple_of((2 * h + c) * PEER_HALF, PEER_HALF)
                q = qt_scr[pl.ds(row0, PEER_HALF), lanes]
                s = jnp.dot(sk_ref[c], q.astype(BF16), preferred_element_type=F32)
                t = [s[v * 8:(v + 1) * 8, :] for v in range(PEER_NKEYS // 8)]
                x = list(t)
                for i, j in _SORT16:
                    _exchange(x, i, j)
                tiles.append(t)
                tops.append(_merge_top16_across_sublanes(x, (4, 2, 1)))
            v0, v1 = tops

            cand = [[v0[a] + v1[b] for b in range(_PAIR_COUNTS[a])] for a in range(PEER_TOPK)]
            flat = [cand[a][b] for a in range(PEER_TOPK) for b in range(_PAIR_COUNTS[a])]
            packed = []
            for v in range(_PAIR_ROWS // 8):
                p = jnp.full((8, LANES), neg_inf, F32)
                for r in range(8):
                    if v * 8 + r < _N_PAIRS:
                        p = jnp.where(sub_is[r], flat[v * 8 + r], p)
                packed.append(p)
            x = packed + [jnp.full((8, LANES), neg_inf, F32)]
            for i, j in _SORT8:
                _exchange(x, i, j)
            y = [pltpu.roll(v, 4, 0) for v in x]
            x = x + y[::-1]
            _bitonic_merge(x)
            x = _merge_top16_across_sublanes(x, (2, 1))
            tau = x[PEER_TOPK - 1]
            bad = bad + jnp.where(_count_ge(packed, tau) != float(PEER_TOPK), 1.0, 0.0)

            top = flat[0]
            z = None
            for p in packed:
                e = jnp.where(p >= tau, jnp.exp(p - top), 0.0)
                z = e if z is None else z + e
            for shift in (4, 2, 1):
                z = z + pltpu.roll(z, shift, 0)
            inv_z = 1.0 / z

            counts = []
            for a in range(PEER_TOPK):
                n_a = None
                for b in range(_PAIR_COUNTS[a]):
                    c = jnp.where(cand[a][b] >= tau, 1.0, 0.0)
                    n_a = c if n_a is None else n_a + c
                counts.append(n_a)

            over0 = _count_ge(tiles[0], v0[PEER_TOPK - 1]) > float(PEER_TOPK)
            over1 = _count_ge(tiles[1], v1[PEER_TOPK - 1]) > float(PEER_TOPK)
            bad = bad + jnp.where(over0, counts[PEER_TOPK - 1], 0.0)
            for a in range(PEER_TOPK - 1):
                bad = bad + jnp.where(v0[a] == v0[a + 1], jnp.abs(counts[a] - counts[a + 1]), 0.0)
            for a in range(PEER_TOPK):
                edge = _PAIR_COUNTS[a]
                full = counts[a] == float(edge)
                if edge < PEER_TOPK:
                    tie = v1[edge - 1] == v1[edge]
                else:
                    tie = over1
                bad = bad + jnp.where(tie, jnp.where(full, 1.0, 0.0), 0.0)

            for v in range(PEER_NKEYS // 8):
                rows_ = slice(v * 8, (v + 1) * 8)
                s0 = tiles[0][v]
                s1 = tiles[1][v]
                n_dense = jnp.zeros((8, LANES), F32)
                rank1 = jnp.zeros((8, LANES), F32)
                for a in range(PEER_TOPK):
                    n_dense = jnp.where(s0 == v0[a], counts[a], n_dense)
                    rank1 = rank1 + jnp.where(v1[a] > s1, 1.0, 0.0)
                an_ref[0, h, rows_, lanes] = jnp.exp(s0 - v0[0]) * inv_z
                an_ref[1, h, rows_, lanes] = n_dense
                br_ref[0, h, rows_, lanes] = jnp.exp(s1 - v1[0]).astype(BF16)
                br_ref[1, h, rows_, lanes] = rank1.astype(BF16)
        return bad

    return lax.fori_loop(0, PEER_HEADS, head_body, jnp.zeros((8, LANES), F32))


def _peer_topk_exact(sk_ref, an_ref, br_ref, qt_scr, vals_scr, rank_scr, e_scr, cand_scr, sel_scr, tq):
    neg_inf = jnp.float32(-jnp.inf)
    n_lt = tq // LANES
    iota_k = lax.broadcasted_iota(jnp.int32, (PEER_NKEYS, LANES), 0).astype(F32)

    def half_body(hc, _):
        q = qt_scr[pl.ds(pl.multiple_of(hc * PEER_HALF, PEER_HALF), PEER_HALF), :]
        s = jnp.dot(sk_ref[hc % 2], q.astype(BF16), preferred_element_type=F32)
        for lt in range(n_lt):
            lanes = slice(lt * LANES, (lt + 1) * LANES)
            s_l = s[:, lanes]

            work = s_l
            rank = jnp.full((PEER_NKEYS, LANES), float(PEER_NKEYS), F32)
            for r in range(PEER_TOPK):
                m = jnp.max(work, axis=0, keepdims=True)
                pos = jnp.min(jnp.where(work == m, iota_k, float(PEER_NKEYS)), axis=0, keepdims=True)
                hit = iota_k == pos
                vals_scr[hc, r:r + 1, lanes] = m
                work = jnp.where(hit, neg_inf, work)
                rank = jnp.where(hit, float(r), rank)
            top = vals_scr[hc, 0:1, lanes]
            rank_scr[hc, :, lanes] = rank
            e_scr[hc, :, lanes] = jnp.where(rank < PEER_TOPK, jnp.exp(s_l - top), 0.0)
        return 0

    lax.fori_loop(0, 2 * PEER_HEADS, half_body, 0)

    iota_p = lax.broadcasted_iota(jnp.int32, (_PAIR_ROWS, LANES), 0).astype(F32)

    def head_body(h, _):
        for lt in range(n_lt):
            lanes = slice(lt * LANES, (lt + 1) * LANES)
            v0 = vals_scr[2 * h, :, lanes]
            v1 = vals_scr[2 * h + 1, :, lanes]
            cand_scr[...] = jnp.full((_PAIR_ROWS, LANES), neg_inf, F32)
            for a in range(PEER_TOPK):
                cand_scr[_PAIR_OFFS[a]:_PAIR_OFFS[a] + _PAIR_COUNTS[a], :] = (
                    v0[a:a + 1, :] + v1[0:_PAIR_COUNTS[a], :])
            cand = cand_scr[...]

            def extract(r, carry):
                work, sel = carry
                m = jnp.max(work, axis=0, keepdims=True)
                pos = jnp.min(jnp.where(work == m, iota_p, float(_PAIR_ROWS)), axis=0, keepdims=True)
                hit = iota_p == pos
                return jnp.where(hit, neg_inf, work), jnp.where(hit, 1.0, sel)

            _, sel = lax.fori_loop(0, PEER_TOPK, extract, (cand, jnp.zeros((_PAIR_ROWS, LANES), F32)))
            sel_scr[...] = sel
            top = cand_scr[0:1, :]
            z = jnp.sum(jnp.where(sel > 0.0, jnp.exp(cand - top), 0.0), axis=0, keepdims=True)
            inv_z = 1.0 / z
            rank0 = rank_scr[2 * h, :, lanes]
            n_dense = jnp.zeros((PEER_NKEYS, LANES), F32)
            for a in range(PEER_TOPK):
                n_a = jnp.sum(sel_scr[_PAIR_OFFS[a]:_PAIR_OFFS[a] + _PAIR_COUNTS[a], :], axis=0, keepdims=True)
                n_dense = jnp.where(rank0 == float(a), n_a, n_dense)
            an_ref[0, h, :, lanes] = e_scr[2 * h, :, lanes] * inv_z
            an_ref[1, h, :, lanes] = n_dense
            br_ref[0, h, :, lanes] = e_scr[2 * h + 1, :, lanes].astype(BF16)
            br_ref[1, h, :, lanes] = rank_scr[2 * h + 1, :, lanes].astype(BF16)
        return 0

    lax.fori_loop(0, PEER_HEADS, head_body, 0)


def _peer_topk_kernel(x_ref, wqt_ref, sk_ref, an_ref, br_ref, qt_scr, vals_scr, rank_scr, e_scr, cand_scr,
                      sel_scr, *, tq):
    qt_scr[...] = lax.dot_general(wqt_ref[...], x_ref[...].astype(BF16), (((1,), (1,)), ((), ())),
                                  preferred_element_type=F32)
    bad = _peer_topk_fast(sk_ref, an_ref, br_ref, qt_scr, tq)

    @pl.when(jnp.max(bad) > 0.0)
    def _():
        _peer_topk_exact(sk_ref, an_ref, br_ref, qt_scr, vals_scr, rank_scr, e_scr, cand_scr, sel_scr, tq)


def _peer_topk(x, wqt, sk, tq):
    rows = x.shape[0]
    return pl.pallas_call(
        functools.partial(_peer_topk_kernel, tq=tq),
        grid=(rows // tq,),
        in_specs=[pl.BlockSpec((tq, D_MODEL), lambda i: (i, 0)), _const_spec(wqt.shape), _const_spec(sk.shape)],
        out_specs=[pl.BlockSpec((2, PEER_HEADS, PEER_NKEYS, tq), lambda i: (0, 0, 0, i)),
                   pl.BlockSpec((2, PEER_HEADS, PEER_NKEYS, tq), lambda i: (0, 0, 0, i))],
        out_shape=[jax.ShapeDtypeStruct((2, PEER_HEADS, PEER_NKEYS, rows), F32),
                   jax.ShapeDtypeStruct((2, PEER_HEADS, PEER_NKEYS, rows), BF16)],
        scratch_shapes=[pltpu.VMEM((2 * PEER_HEADS * PEER_HALF, tq), F32),
                        pltpu.VMEM((2 * PEER_HEADS, PEER_TOPK, tq), F32),
                        pltpu.VMEM((2 * PEER_HEADS, PEER_NKEYS, tq), F32),
                        pltpu.VMEM((2 * PEER_HEADS, PEER_NKEYS, tq), F32),
                        pltpu.VMEM((_PAIR_ROWS, LANES), F32),
                        pltpu.VMEM((_PAIR_ROWS, LANES), F32)],
        compiler_params=_cparams("arbitrary"),
        name="peer_topk",
    )(x, wqt, sk)


PEER_CHUNK = 8 * PEER_NKEYS
PEER_N_CHUNKS = PEER_EXPERTS // PEER_CHUNK


PEER_STEP_CHUNKS = 2


def _peer_dense_kernel(x_ref, an_ref, br_ref, u_ref, vt_ref, o_ref, xt_scr, acc_scr, br_scr, act_scr, wt_scr,
                       *, tq):
    e = pl.program_id(1)

    @pl.when(e == 0)
    def _():
        xt_scr[...] = x_ref[...].T.astype(BF16)
        acc_scr[...] = jnp.zeros_like(acc_scr)
        br_scr[...] = br_ref[...]

    def up(q):
        rows_ = slice(2 * q * PEER_NKEYS, 2 * (q + 1) * PEER_NKEYS)
        act_scr[rows_, :] = jnp.dot(u_ref[rows_, :], xt_scr[...], preferred_element_type=F32)

    def gate_key(ii):
        rows_ = slice(ii * PEER_NKEYS, (ii + 1) * PEER_NKEYS)
        for lt in range(tq // LANES):
            lanes = slice(lt * LANES, (lt + 1) * LANES)
            gate = None
            for h in range(PEER_HEADS):
                a = jnp.broadcast_to(an_ref[0, h, ii:ii + 1, lanes], (PEER_NKEYS, LANES)).astype(BF16)
                n = jnp.broadcast_to(an_ref[1, h, ii:ii + 1, lanes], (PEER_NKEYS, LANES)).astype(BF16)
                term = a * jnp.where(br_scr[1, h, :, lanes] < n, br_scr[0, h, :, lanes], 0)
                gate = term if gate is None else gate + term
            wt_scr[rows_, lanes] = gate * jax.nn.gelu(act_scr[rows_, lanes]).astype(BF16)

    def down(c):
        cols = slice(c * PEER_CHUNK, (c + 1) * PEER_CHUNK)
        acc_scr[...] += jnp.dot(vt_ref[:, cols], wt_scr[cols, :], preferred_element_type=F32)

    n_up = 4 * PEER_STEP_CHUNKS
    up(0)
    for q in range(n_up):
        if q + 1 < n_up:
            up(q + 1)
        gate_key(2 * q)
        gate_key(2 * q + 1)
        if q % 4 == 3:
            down(q // 4)

    @pl.when(e == pl.num_programs(1) - 1)
    def _():
        o_ref[...] = acc_scr[...].T


def _peer_dense(x, an, br, u16, vt16, tq):
    rows = x.shape[0]
    step = PEER_STEP_CHUNKS * PEER_CHUNK
    return pl.pallas_call(
        functools.partial(_peer_dense_kernel, tq=tq),
        grid=(rows // tq, PEER_EXPERTS // step),
        in_specs=[pl.BlockSpec((tq, D_MODEL), lambda t, e: (t, 0)),
                  pl.BlockSpec((2, PEER_HEADS, 8 * PEER_STEP_CHUNKS, tq), lambda t, e: (0, 0, e, t)),
                  pl.BlockSpec((2, PEER_HEADS, PEER_NKEYS, tq), lambda t, e: (0, 0, 0, t)),
                  pl.BlockSpec((step, D_MODEL), lambda t, e: (e, 0)),
                  pl.BlockSpec((D_MODEL, step), lambda t, e: (0, e))],
        out_specs=pl.BlockSpec((tq, D_MODEL), lambda t, e: (t, 0)),
        out_shape=jax.ShapeDtypeStruct((rows, D_MODEL), F32),
        scratch_shapes=[pltpu.VMEM((D_MODEL, tq), BF16),
                        pltpu.VMEM((D_MODEL, tq), F32),
                        pltpu.VMEM((2, PEER_HEADS, PEER_NKEYS, tq), BF16),
                        pltpu.VMEM((step, tq), F32),
                        pltpu.VMEM((step, tq), BF16)],
        compiler_params=_cparams("arbitrary", "arbitrary"),
        name="peer_dense",
    )(x, an, br, u16, vt16)


def _ln_ple_kernel(x_ref, peer_ref, p_ref, lng_ref, lnb_ref, wp_ref, wg_ref, o_ref):
    x2 = _layernorm(ALPHA * x_ref[...] + peer_ref[...], lng_ref[...], lnb_ref[...])
    o_ref[...] = x2 + _mm(p_ref[...], wp_ref[...]) * jax.nn.sigmoid(_mm(x2, wg_ref[...]))


def _ln_ple(x, peer, p, lng, lnb, wp, wg, tq):
    rows = x.shape[0]
    tok = pl.BlockSpec((tq, D_MODEL), lambda i: (i, 0))
    return pl.pallas_call(
        _ln_ple_kernel,
        grid=(rows // tq,),
        in_specs=[tok, tok, pl.BlockSpec((tq, PLE_DIM), lambda i: (i, 0)), _const_spec(lng.shape),
                  _const_spec(lnb.shape), _const_spec(wp.shape), _const_spec(wg.shape)],
        out_specs=tok,
        out_shape=jax.ShapeDtypeStruct((rows, D_MODEL), F32),
        compiler_params=_cparams("arbitrary"),
        name="ln_ple",
    )(x, peer, p, lng, lnb, wp, wg)


def _token_tile(rows, want):
    return math.gcd(rows, want)


def _trunk(x, p, s5, rg, ln, peer, ple):
    batch, seq, _ = x.shape
    rows = batch * seq
    ln1_g, ln1_b, ln2_g, ln2_b = ln
    peer_w_q, peer_subkeys, peer_u, peer_v = peer
    ple_w_proj, ple_w_gate = ple
    xt = jnp.transpose(x, (1, 0, 2)).reshape(rows, D_MODEL)
    pt = jnp.transpose(p, (0, 2, 1, 3)).reshape(DEPTH, rows, PLE_DIM)
    for i in range(DEPTH):
        j = i // 2
        if i % 2 == 0:
            xt = _s5_layer(xt, batch, *(w[j] for w in s5), ln1_g[i], ln1_b[i])
        else:
            xt = _rg_layer(xt, batch, *(w[j] for w in rg), ln1_g[i], ln1_b[i])
        an, br = _peer_topk(xt, peer_w_q[i].T.astype(BF16), peer_subkeys[i].astype(BF16), _token_tile(rows, 256))
        peer_out = _peer_dense(xt, an, br, peer_u[i].astype(BF16), peer_v[i].T.astype(BF16),
                               _token_tile(rows, 512))
        xt = _ln_ple(xt, peer_out, pt[i], ln2_g[i][None, :], ln2_b[i][None, :],
                     ple_w_proj[i].astype(BF16), ple_w_gate[i].astype(BF16), _token_tile(rows, 512))
    return jnp.transpose(xt.reshape(seq, batch, D_MODEL), (1, 0, 2))


def kernel(x_prompt, x_sample, p_prompt, p_sample, s5_w_in, s5_lam_re, s5_lam_im, s5_log_step, s5_b_re, s5_b_im, s5_c_re, s5_c_im, s5_d, s5_w_glu, rg_w_in, rg_conv_w, rg_conv_b, rg_w_gate_a, rg_b_gate_a, rg_w_gate_x, rg_b_gate_x, rg_lambda, rg_w_out, ln1_g, ln1_b, ln2_g, ln2_b, peer_w_q, peer_subkeys, peer_u, peer_v, ple_w_proj, ple_w_gate):
    s5 = (s5_w_in, s5_lam_re, s5_lam_im, s5_log_step, s5_b_re, s5_b_im, s5_c_re, s5_c_im, s5_d, s5_w_glu)
    rg = (rg_w_in, rg_conv_w, rg_conv_b, rg_w_gate_a, rg_b_gate_a, rg_w_gate_x, rg_b_gate_x, rg_lambda, rg_w_out)
    ln = (ln1_g, ln1_b, ln2_g, ln2_b)
    peer = (peer_w_q, peer_subkeys, peer_u, peer_v)
    ple = (ple_w_proj, ple_w_gate)
    y_prompt = _trunk(x_prompt, p_prompt, s5, rg, ln, peer, ple)
    y_sample = _trunk(x_sample, p_sample, s5, rg, ln, peer, ple)
    return (y_prompt, y_sample)
```

```python
import functools
import math

import jax
import jax.numpy as jnp
from jax import lax
from jax.experimental import pallas as pl
from jax.experimental.pallas import tpu as pltpu

F32 = jnp.float32
BF16 = jnp.bfloat16

D_MODEL = 1024
DEPTH = 2
S5_GROUP = 16
S5_GROUPS = 64
S5_STATE = 64
S5_SLABS = 4
S5_SLAB_CH = D_MODEL // S5_SLABS
S5_SLAB_ST = 16 * S5_STATE
RG_BLOCKS = 4
RG_BLOCK = 256
RG_CONV = 4
RG_C = 8.0
PEER_HEADS = 8
PEER_NKEYS = 128
PEER_EXPERTS = PEER_NKEYS * PEER_NKEYS
PEER_HALF = 128
PEER_TOPK = 16
PLE_DIM = 256
ALPHA = (2 * DEPTH) ** 0.25
LN_EPS = 1e-5

LANES = 128
VMEM_LIMIT = 56 * 1024 * 1024
MIX_ROWS = 512
RG_ROWS = 1024
SCAN_COLS = 512

_PAIR_COUNTS = [min(PEER_TOPK, PEER_TOPK // (a + 1)) for a in range(PEER_TOPK)]
_PAIR_OFFS = [sum(_PAIR_COUNTS[:a]) for a in range(PEER_TOPK)]
_N_PAIRS = sum(_PAIR_COUNTS)
_PAIR_ROWS = ((_N_PAIRS + 7) // 8) * 8


def _mm(a, b):
    return jnp.dot(a.astype(BF16), b.astype(BF16), preferred_element_type=F32)


def _layernorm(z, g, b):
    mu = jnp.mean(z, axis=-1, keepdims=True)
    zc = z - mu
    var = jnp.mean(zc * zc, axis=-1, keepdims=True)
    return zc * lax.rsqrt(var + LN_EPS) * g + b


def _cparams(*sem):
    return pltpu.CompilerParams(dimension_semantics=sem, vmem_limit_bytes=VMEM_LIMIT)


def _const_spec(shape):
    nd = len(shape)
    return pl.BlockSpec(shape, lambda *_: (0,) * nd)


def _s5_kernel(*refs, batch, steps, reverse, final):
    if final:
        (x_ref, win_ref, wb_ref, wc_ref, a_ref, yprev_ref, d_ref, wglu_ref, lng_ref, lnb_ref,
         out_ref, bu_scr, h_scr) = refs
    else:
        x_ref, win_ref, wb_ref, wc_ref, a_ref, out_ref, bu_scr, h_scr = refs

    @pl.when(pl.program_id(0) == 0)
    def _():
        h_scr[...] = jnp.zeros_like(h_scr)

    x = x_ref[...]
    u = _mm(x, win_ref[...])
    ub = u.astype(BF16)
    for s in range(S5_SLABS):
        bu_scr[:, s * 2 * S5_SLAB_ST:(s + 1) * 2 * S5_SLAB_ST] = jnp.dot(
            ub[:, s * S5_SLAB_CH:(s + 1) * S5_SLAB_CH], wb_ref[s], preferred_element_type=F32)

    for s in range(S5_SLABS):
        for cb in range(S5_SLAB_ST // SCAN_COLS):
            c_re = s * 2 * S5_SLAB_ST + cb * SCAN_COLS
            c_im = c_re + S5_SLAB_ST
            a_re = jnp.broadcast_to(a_ref[s:s + 1, cb * SCAN_COLS:(cb + 1) * SCAN_COLS], (batch, SCAN_COLS))
            a_im = jnp.broadcast_to(
                a_ref[S5_SLABS + s:S5_SLABS + s + 1, cb * SCAN_COLS:(cb + 1) * SCAN_COLS], (batch, SCAN_COLS))

            def step(k, carry, c_re=c_re, c_im=c_im, a_re=a_re, a_im=a_im):
                h_re, h_im = carry
                t = (steps - 1 - k) if reverse else k
                row = pl.multiple_of(t * batch, batch)
                b_re = bu_scr[pl.ds(row, batch), c_re:c_re + SCAN_COLS]
                b_im = bu_scr[pl.ds(row, batch), c_im:c_im + SCAN_COLS]
                n_re = a_re * h_re - a_im * h_im + b_re
                n_im = a_re * h_im + a_im * h_re + b_im
                bu_scr[pl.ds(row, batch), c_re:c_re + SCAN_COLS] = n_re
                bu_scr[pl.ds(row, batch), c_im:c_im + SCAN_COLS] = n_im
                return n_re, n_im

            h_re, h_im = lax.fori_loop(
                0, steps, step,
                (h_scr[:, c_re:c_re + SCAN_COLS], h_scr[:, c_im:c_im + SCAN_COLS]), unroll=4)
            h_scr[:, c_re:c_re + SCAN_COLS] = h_re
            h_scr[:, c_im:c_im + SCAN_COLS] = h_im

    ys = []
    for s in range(S5_SLABS):
        hb = bu_scr[:, s * 2 * S5_SLAB_ST:(s + 1) * 2 * S5_SLAB_ST].astype(BF16)
        ys.append(jnp.dot(hb, wc_ref[s], preferred_element_type=F32))
    y = jnp.concatenate(ys, axis=1)

    if not final:
        out_ref[...] = y
        return

    y = y + yprev_ref[...] + d_ref[...] * u
    hg = jax.nn.gelu(y)
    vg = _mm(hg, wglu_ref[...])
    mix = vg[:, :D_MODEL] * jax.nn.sigmoid(vg[:, D_MODEL:])
    out_ref[...] = _layernorm(ALPHA * x + mix, lng_ref[...], lnb_ref[...])


def _s5_discretize(lam_re, lam_im, log_step, b_re, b_im):
    step = jnp.exp(log_step.astype(F32))[:, None]
    lr = lam_re.astype(F32)
    li = lam_im.astype(F32)
    mag = jnp.exp(lr * step)
    ang = li * step
    ar = mag * jnp.cos(ang)
    ai = mag * jnp.sin(ang)
    den = lr * lr + li * li
    zr = ar - 1.0
    qr = (zr * lr + ai * li) / den
    qi = (ai * lr - zr * li) / den
    br = b_re.astype(F32)
    bi = b_im.astype(F32)
    bbr = qr[..., None] * br - qi[..., None] * bi
    bbi = qr[..., None] * bi + qi[..., None] * br
    return ar, ai, bbr, bbi


def _s5_weights(lam_re, lam_im, log_step, b_re, b_im, c_re, c_im):
    ar, ai, bbr, bbi = _s5_discretize(lam_re, lam_im, log_step, b_re, b_im)
    eye = jnp.eye(16, dtype=F32)

    def in_blocks(bb):
        bb4 = bb.reshape(S5_SLABS, 16, S5_STATE, S5_GROUP)
        return jnp.einsum('sgpc,gh->sgchp', bb4, eye).reshape(S5_SLABS, S5_SLAB_CH, S5_SLAB_ST)

    def out_blocks(cc):
        cc4 = cc.astype(F32).reshape(S5_SLABS, 16, S5_GROUP, S5_STATE)
        return jnp.einsum('sgcp,gh->sgphc', cc4, eye).reshape(S5_SLABS, S5_SLAB_ST, S5_SLAB_CH)

    wb = jnp.concatenate([in_blocks(bbr), in_blocks(bbi)], axis=2).astype(BF16)
    wc = jnp.concatenate([out_blocks(c_re), -out_blocks(c_im)], axis=1).astype(BF16)
    avec = jnp.concatenate([ar.reshape(S5_SLABS, S5_SLAB_ST), ai.reshape(S5_SLABS, S5_SLAB_ST)], axis=0)
    return wb, wc, avec


def _s5_call(x, win, wb, wc, avec, batch, reverse, tail):
    rows = x.shape[0]
    n = rows // MIX_ROWS
    steps = MIX_ROWS // batch
    final = tail is not None
    if reverse:
        tok = pl.BlockSpec((MIX_ROWS, D_MODEL), lambda i: (n - 1 - i, 0))
    else:
        tok = pl.BlockSpec((MIX_ROWS, D_MODEL), lambda i: (i, 0))
    args = [x, win, wb, wc, avec]
    specs = [tok, _const_spec(win.shape), _const_spec(wb.shape), _const_spec(wc.shape), _const_spec(avec.shape)]
    if final:
        yprev, dskip, wglu, lng, lnb = tail
        args += [yprev, dskip, wglu, lng, lnb]
        specs += [tok, _const_spec(dskip.shape), _const_spec(wglu.shape), _const_spec(lng.shape),
                  _const_spec(lnb.shape)]
    return pl.pallas_call(
        functools.partial(_s5_kernel, batch=batch, steps=steps, reverse=reverse, final=final),
        grid=(n,),
        in_specs=specs,
        out_specs=tok,
        out_shape=jax.ShapeDtypeStruct((rows, D_MODEL), F32),
        scratch_shapes=[pltpu.VMEM((MIX_ROWS, S5_SLABS * 2 * S5_SLAB_ST), F32),
                        pltpu.VMEM((batch, S5_SLABS * 2 * S5_SLAB_ST), F32)],
        compiler_params=_cparams("arbitrary"),
        name="s5_bwd_glu_ln" if final else "s5_fwd",
    )(*args)


def _s5_layer(x, batch, w_in, lam_re, lam_im, log_step, b_re, b_im, c_re, c_im, d_skip, w_glu, ln_g, ln_b):
    win = w_in.astype(BF16)
    wb0, wc0, av0 = _s5_weights(lam_re[0], lam_im[0], log_step[0], b_re[0], b_im[0], c_re[0], c_im[0])
    wb1, wc1, av1 = _s5_weights(lam_re[1], lam_im[1], log_step[1], b_re[1], b_im[1], c_re[1], c_im[1])
    y_fwd = _s5_call(x, win, wb0, wc0, av0, batch, False, None)
    tail = (y_fwd, d_skip.astype(F32)[None, :], w_glu.astype(BF16), ln_g[None, :], ln_b[None, :])
    return _s5_call(x, win, wb1, wc1, av1, batch, True, tail)


def _rg_kernel(*refs, batch, steps, reverse, final, n_chunks):
    if final:
        (x_ref, xp_ref, xn_ref, wr_ref, cw_ref, cb_ref, wa_ref, ba_ref, wx_ref, bx_ref, sp_ref,
         hprev_ref, wg_ref, wo_ref, lng_ref, lnb_ref, out_ref, xe_scr, a_scr, b_scr, h_scr) = refs
    else:
        (x_ref, xp_ref, xn_ref, wr_ref, cw_ref, cb_ref, wa_ref, ba_ref, wx_ref, bx_ref, sp_ref,
         out_ref, xe_scr, a_scr, b_scr, h_scr) = refs
    rows = steps * batch
    i = pl.program_id(0)
    chunk = (n_chunks - 1 - i) if reverse else i

    @pl.when(i == 0)
    def _():
        h_scr[...] = jnp.zeros_like(h_scr)

    x = x_ref[...]
    keep_prev = (chunk > 0).astype(F32)
    keep_next = (chunk < n_chunks - 1).astype(F32)
    xe_scr[0:batch, :] = xp_ref[...] * keep_prev
    xe_scr[batch:batch + rows, :] = x
    xe_scr[batch + rows:, :] = xn_ref[...] * keep_next
    r_ext = _mm(xe_scr[...], wr_ref[...])
    xe_scr[...] = r_ext

    c = cb_ref[...] + xe_scr[0:rows, :] * cw_ref[0:1, :]
    for k in range(1, RG_CONV):
        c = c + xe_scr[k * batch:k * batch + rows, :] * cw_ref[k:k + 1, :]

    cb16 = c.astype(BF16)
    ga = []
    gx = []
    for blk in range(RG_BLOCKS):
        cs = cb16[:, blk * RG_BLOCK:(blk + 1) * RG_BLOCK]
        ga.append(jnp.dot(cs, wa_ref[blk], preferred_element_type=F32))
        gx.append(jnp.dot(cs, wx_ref[blk], preferred_element_type=F32))
    r_gate = jax.nn.sigmoid(jnp.concatenate(ga, axis=1) + ba_ref[...])
    i_gate = jax.nn.sigmoid(jnp.concatenate(gx, axis=1) + bx_ref[...])
    log_a = -RG_C * r_gate * sp_ref[...]
    a_scr[...] = jnp.exp(log_a)
    th = jnp.tanh(log_a)
    b_scr[...] = jnp.sqrt(-2.0 * th / (1.0 - th)) * (i_gate * c)

    def step(k, h):
        t = (steps - 1 - k) if reverse else k
        row = pl.multiple_of(t * batch, batch)
        h = a_scr[pl.ds(row, batch), :] * h + b_scr[pl.ds(row, batch), :]
        b_scr[pl.ds(row, batch), :] = h
        return h

    h_scr[...] = lax.fori_loop(0, steps, step, h_scr[...], unroll=4)

    if not final:
        out_ref[...] = b_scr[...]
        return

    h_tot = b_scr[...] + hprev_ref[...]
    g = _mm(x, wg_ref[...])
    y = h_tot * jax.nn.gelu(g)
    mix = _mm(y, wo_ref[...])
    out_ref[...] = _layernorm(ALPHA * x + mix, lng_ref[...], lnb_ref[...])


def _rg_call(x, wr, cw, cb, wa, ba, wx, bx, sp, batch, reverse, tail):
    rows = x.shape[0]
    chunk = math.gcd(rows, RG_ROWS)
    n = rows // chunk
    steps = chunk // batch
    final = tail is not None
    per_prev = chunk // batch
    per_next = chunk // (2 * batch)
    last_next = rows // (2 * batch) - 1

    def cidx(i):
        return (n - 1 - i) if reverse else i

    tok = pl.BlockSpec((chunk, D_MODEL), lambda i: (cidx(i), 0))
    prev = pl.BlockSpec((batch, D_MODEL), lambda i: (jnp.maximum(cidx(i) * per_prev - 1, 0), 0))
    nxt = pl.BlockSpec((2 * batch, D_MODEL), lambda i: (jnp.minimum((cidx(i) + 1) * per_next, last_next), 0))
    args = [x, x, x, wr, cw, cb, wa, ba, wx, bx, sp]
    specs = [tok, prev, nxt] + [_const_spec(a.shape) for a in args[3:]]
    if final:
        args += list(tail)
        specs += [tok] + [_const_spec(a.shape) for a in tail[1:]]
    return pl.pallas_call(
        functools.partial(_rg_kernel, batch=batch, steps=steps, reverse=reverse, final=final, n_chunks=n),
        grid=(n,),
        in_specs=specs,
        out_specs=tok,
        out_shape=jax.ShapeDtypeStruct((rows, D_MODEL), F32),
        scratch_shapes=[pltpu.VMEM((chunk + 3 * batch, D_MODEL), F32),
                        pltpu.VMEM((chunk, D_MODEL), F32),
                        pltpu.VMEM((chunk, D_MODEL), F32),
                        pltpu.VMEM((batch, D_MODEL), F32)],
        compiler_params=_cparams("arbitrary"),
        name="rg_bwd_out_ln" if final else "rg_fwd",
    )(*args)


def _rg_layer(x, batch, w_in, conv_w, conv_b, w_ga, b_ga, w_gx, b_gx, lam, w_out, ln_g, ln_b):
    wg = w_in[:, :D_MODEL].astype(BF16)
    wr = w_in[:, D_MODEL:].astype(BF16)
    cw = jnp.concatenate([conv_w.astype(F32), jnp.zeros((8 - RG_CONV, D_MODEL), F32)], axis=0)
    cb = conv_b.astype(F32)[None, :]
    sp = jax.nn.softplus(-lam.astype(F32))
    common = lambda d: (wr, cw, cb, w_ga[d].astype(BF16), b_ga[d][None, :].astype(F32),
                        w_gx[d].astype(BF16), b_gx[d][None, :].astype(F32), sp[d][None, :])
    h_fwd = _rg_call(x, *common(0), batch, False, None)
    tail = (h_fwd, wg, w_out.astype(BF16), ln_g[None, :], ln_b[None, :])
    return _rg_call(x, *common(1), batch, True, tail)


def _sorting_pairs(n):
    pairs = []
    p = 1
    while p < n:
        k = p
        while k >= 1:
            for j in range(k % p, n - k, 2 * k):
                for i in range(min(k, n - j - k)):
                    if (i + j) // (2 * p) == (i + j + k) // (2 * p):
                        pairs.append((i + j, i + j + k))
            k //= 2
        p *= 2
    return pairs


_SORT16 = _sorting_pairs(16)
_SORT8 = _sorting_pairs(8)


def _exchange(x, i, j):
    x[i], x[j] = jnp.maximum(x[i], x[j]), jnp.minimum(x[i], x[j])


def _bitonic_merge(x):
    d = len(x) // 2
    while d >= 1:
        for i in range(len(x)):
            if i & d == 0:
                _exchange(x, i, i + d)
        d //= 2


def _merge_top16_across_sublanes(x, shifts):
    for shift in shifts:
        y = [pltpu.roll(v, shift, 0) for v in x]
        x = [jnp.maximum(x[k], y[PEER_TOPK - 1 - k]) for k in range(PEER_TOPK)]
        _bitonic_merge(x)
    return x


def _count_ge(tiles, thr):
    cnt = None
    for t in tiles:
        c = jnp.where(t >= thr, 1.0, 0.0)
        cnt = c if cnt is None else cnt + c
    for shift in (4, 2, 1):
        cnt = cnt + pltpu.roll(cnt, shift, 0)
    return cnt


def _peer_topk_fast(sk_ref, an_ref, br_ref, qt_scr, tq):
    neg_inf = jnp.float32(-jnp.inf)
    sub_iota = lax.broadcasted_iota(jnp.int32, (8, LANES), 0)
    sub_is = [sub_iota == r for r in range(8)]

    def head_body(h, bad):
        for lt in range(tq // LANES):
            lanes = slice(lt * LANES, (lt + 1) * LANES)
            tiles = []
            tops = []
            for c in range(2):
                row0 = pl.multiple_of((2 * h + c) * PEER_HALF, PEER_HALF)
                q = qt_scr[pl.ds(row0, PEER_HALF), lanes]
                s = jnp.dot(sk_ref[c], q.astype(BF16), preferred_element_type=F32)
                t = [s[v * 8:(v + 1) * 8, :] for v in range(PEER_NKEYS // 8)]
                x = list(t)
                for i, j in _SORT16:
                    _exchange(x, i, j)
                tiles.append(t)
                tops.append(_merge_top16_across_sublanes(x, (4, 2, 1)))
            v0, v1 = tops

            cand = [[v0[a] + v1[b] for b in range(_PAIR_COUNTS[a])] for a in range(PEER_TOPK)]
            flat = [cand[a][b] for a in range(PEER_TOPK) for b in range(_PAIR_COUNTS[a])]
            packed = []
            for v in range(_PAIR_ROWS // 8):
                p = jnp.full((8, LANES), neg_inf, F32)
                for r in range(8):
                    if v * 8 + r < _N_PAIRS:
                        p = jnp.where(sub_is[r], flat[v * 8 + r], p)
                packed.append(p)
            x = packed + [jnp.full((8, LANES), neg_inf, F32)]
            for i, j in _SORT8:
                _exchange(x, i, j)
            y = [pltpu.roll(v, 4, 0) for v in x]
            x = x + y[::-1]
            _bitonic_merge(x)
            x = _merge_top16_across_sublanes(x, (2, 1))
            tau = x[PEER_TOPK - 1]
            bad = bad + jnp.where(_count_ge(packed, tau) != float(PEER_TOPK), 1.0, 0.0)

            top = flat[0]
            z = None
            for p in packed:
                e = jnp.where(p >= tau, jnp.exp(p - top), 0.0)
                z = e if z is None else z + e
            for shift in (4, 2, 1):
                z = z + pltpu.roll(z, shift, 0)
            inv_z = 1.0 / z

            counts = []
            for a in range(PEER_TOPK):
                n_a = None
                for b in range(_PAIR_COUNTS[a]):
                    c = jnp.where(cand[a][b] >= tau, 1.0, 0.0)
                    n_a = c if n_a is None else n_a + c
                counts.append(n_a)

            over0 = _count_ge(tiles[0], v0[PEER_TOPK - 1]) > float(PEER_TOPK)
            over1 = _count_ge(tiles[1], v1[PEER_TOPK - 1]) > float(PEER_TOPK)
            bad = bad + jnp.where(over0, counts[PEER_TOPK - 1], 0.0)
            for a in range(PEER_TOPK - 1):
                bad = bad + jnp.where(v0[a] == v0[a + 1], jnp.abs(counts[a] - counts[a + 1]), 0.0)
            for a in range(PEER_TOPK):
                edge = _PAIR_COUNTS[a]
                full = counts[a] == float(edge)
                if edge < PEER_TOPK:
                    tie = v1[edge - 1] == v1[edge]
                else:
                    tie = over1
                bad = bad + jnp.where(tie, jnp.where(full, 1.0, 0.0), 0.0)

            for v in range(PEER_NKEYS // 8):
                rows_ = slice(v * 8, (v + 1) * 8)
                s0 = tiles[0][v]
                s1 = tiles[1][v]
                n_dense = jnp.zeros((8, LANES), F32)
                rank1 = jnp.zeros((8, LANES), F32)
                for a in range(PEER_TOPK):
                    n_dense = jnp.where(s0 == v0[a], counts[a], n_dense)
                    rank1 = rank1 + jnp.where(v1[a] > s1, 1.0, 0.0)
                an_ref[0, h, rows_, lanes] = jnp.exp(s0 - v0[0]) * inv_z
                an_ref[1, h, rows_, lanes] = n_dense
                br_ref[0, h, rows_, lanes] = jnp.exp(s1 - v1[0]).astype(BF16)
                br_ref[1, h, rows_, lanes] = rank1.astype(BF16)
        return bad

    return lax.fori_loop(0, PEER_HEADS, head_body, jnp.zeros((8, LANES), F32))


def _peer_topk_exact(sk_ref, an_ref, br_ref, qt_scr, vals_scr, rank_scr, e_scr, cand_scr, sel_scr, tq):
    neg_inf = jnp.float32(-jnp.inf)
    n_lt = tq // LANES
    iota_k = lax.broadcasted_iota(jnp.int32, (PEER_NKEYS, LANES), 0).astype(F32)

    def half_body(hc, _):
        q = qt_scr[pl.ds(pl.multiple_of(hc * PEER_HALF, PEER_HALF), PEER_HALF), :]
        s = jnp.dot(sk_ref[hc % 2], q.astype(BF16), preferred_element_type=F32)
        for lt in range(n_lt):
            lanes = slice(lt * LANES, (lt + 1) * LANES)
            s_l = s[:, lanes]

            work = s_l
            rank = jnp.full((PEER_NKEYS, LANES), float(PEER_NKEYS), F32)
            for r in range(PEER_TOPK):
                m = jnp.max(work, axis=0, keepdims=True)
                pos = jnp.min(jnp.where(work == m, iota_k, float(PEER_NKEYS)), axis=0, keepdims=True)
                hit = iota_k == pos
                vals_scr[hc, r:r + 1, lanes] = m
                work = jnp.where(hit, neg_inf, work)
                rank = jnp.where(hit, float(r), rank)
            top = vals_scr[hc, 0:1, lanes]
            rank_scr[hc, :, lanes] = rank
            e_scr[hc, :, lanes] = jnp.where(rank < PEER_TOPK, jnp.exp(s_l - top), 0.0)
        return 0

    lax.fori_loop(0, 2 * PEER_HEADS, half_body, 0)

    iota_p = lax.broadcasted_iota(jnp.int32, (_PAIR_ROWS, LANES), 0).astype(F32)

    def head_body(h, _):
        for lt in range(n_lt):
            lanes = slice(lt * LANES, (lt + 1) * LANES)
            v0 = vals_scr[2 * h, :, lanes]
            v1 = vals_scr[2 * h + 1, :, lanes]
            cand_scr[...] = jnp.full((_PAIR_ROWS, LANES), neg_inf, F32)
            for a in range(PEER_TOPK):
                cand_scr[_PAIR_OFFS[a]:_PAIR_OFFS[a] + _PAIR_COUNTS[a], :] = (
                    v0[a:a + 1, :] + v1[0:_PAIR_COUNTS[a], :])
            cand = cand_scr[...]

            def extract(r, carry):
                work, sel = carry
                m = jnp.max(work, axis=0, keepdims=True)
                pos = jnp.min(jnp.where(work == m, iota_p, float(_PAIR_ROWS)), axis=0, keepdims=True)
                hit = iota_p == pos
                return jnp.where(hit, neg_inf, work), jnp.where(hit, 1.0, sel)

            _, sel = lax.fori_loop(0, PEER_TOPK, extract, (cand, jnp.zeros((_PAIR_ROWS, LANES), F32)))
            sel_scr[...] = sel
            top = cand_scr[0:1, :]
            z = jnp.sum(jnp.where(sel > 0.0, jnp.exp(cand - top), 0.0), axis=0, keepdims=True)
            inv_z = 1.0 / z
            rank0 = rank_scr[2 * h, :, lanes]
            n_dense = jnp.zeros((PEER_NKEYS, LANES), F32)
            for a in range(PEER_TOPK):
                n_a = jnp.sum(sel_scr[_PAIR_OFFS[a]:_PAIR_OFFS[a] + _PAIR_COUNTS[a], :], axis=0, keepdims=True)
                n_dense = jnp.where(rank0 == float(a), n_a, n_dense)
            an_ref[0, h, :, lanes] = e_scr[2 * h, :, lanes] * inv_z
            an_ref[1, h, :, lanes] = n_dense
            br_ref[0, h, :, lanes] = e_scr[2 * h + 1, :, lanes].astype(BF16)
            br_ref[1, h, :, lanes] = rank_scr[2 * h + 1, :, lanes].astype(BF16)
        return 0

    lax.fori_loop(0, PEER_HEADS, head_body, 0)


def _peer_topk_kernel(x_ref, wqt_ref, sk_ref, an_ref, br_ref, qt_scr, vals_scr, rank_scr, e_scr, cand_scr,
                      sel_scr, *, tq):
    qt_scr[...] = lax.dot_general(wqt_ref[...], x_ref[...].astype(BF16), (((1,), (1,)), ((), ())),
                                  preferred_element_type=F32)
    bad = _peer_topk_fast(sk_ref, an_ref, br_ref, qt_scr, tq)

    @pl.when(jnp.max(bad) > 0.0)
    def _():
        _peer_topk_exact(sk_ref, an_ref, br_ref, qt_scr, vals_scr, rank_scr, e_scr, cand_scr, sel_scr, tq)


def _peer_topk(x, wqt, sk, tq):
    rows = x.shape[0]
    return pl.pallas_call(
        functools.partial(_peer_topk_kernel, tq=tq),
        grid=(rows // tq,),
        in_specs=[pl.BlockSpec((tq, D_MODEL), lambda i: (i, 0)), _const_spec(wqt.shape), _const_spec(sk.shape)],
        out_specs=[pl.BlockSpec((2, PEER_HEADS, PEER_NKEYS, tq), lambda i: (0, 0, 0, i)),
                   pl.BlockSpec((2, PEER_HEADS, PEER_NKEYS, tq), lambda i: (0, 0, 0, i))],
        out_shape=[jax.ShapeDtypeStruct((2, PEER_HEADS, PEER_NKEYS, rows), F32),
                   jax.ShapeDtypeStruct((2, PEER_HEADS, PEER_NKEYS, rows), BF16)],
        scratch_shapes=[pltpu.VMEM((2 * PEER_HEADS * PEER_HALF, tq), F32),
                        pltpu.VMEM((2 * PEER_HEADS, PEER_TOPK, tq), F32),
                        pltpu.VMEM((2 * PEER_HEADS, PEER_NKEYS, tq), F32),
                        pltpu.VMEM((2 * PEER_HEADS, PEER_NKEYS, tq), F32),
                        pltpu.VMEM((_PAIR_ROWS, LANES), F32),
                        pltpu.VMEM((_PAIR_ROWS, LANES), F32)],
        compiler_params=_cparams("arbitrary"),
        name="peer_topk",
    )(x, wqt, sk)


PEER_CHUNK = 8 * PEER_NKEYS
PEER_N_CHUNKS = PEER_EXPERTS // PEER_CHUNK


PEER_STEP_CHUNKS = 2


def _peer_dense_kernel(x_ref, an_ref, br_ref, u_ref, vt_ref, p_ref, lng_ref, lnb_ref, wp_ref, wg_ref, o_ref,
                       xt_scr, acc_scr, br_scr, act_scr, wt_scr, *, tq):
    e = pl.program_id(1)

    @pl.when(e == 0)
    def _():
        xt_scr[...] = x_ref[...].T.astype(BF16)
        acc_scr[...] = jnp.zeros_like(acc_scr)
        br_scr[...] = br_ref[...]

    def up(q):
        rows_ = slice(2 * q * PEER_NKEYS, 2 * (q + 1) * PEER_NKEYS)
        act_scr[rows_, :] = jnp.dot(u_ref[rows_, :], xt_scr[...], preferred_element_type=F32)

    def gate_key(ii):
        rows_ = slice(ii * PEER_NKEYS, (ii + 1) * PEER_NKEYS)
        for lt in range(tq // LANES):
            lanes = slice(lt * LANES, (lt + 1) * LANES)
            gate = None
            for h in range(PEER_HEADS):
                a = jnp.broadcast_to(an_ref[0, h, ii:ii + 1, lanes], (PEER_NKEYS, LANES)).astype(BF16)
                n = jnp.broadcast_to(an_ref[1, h, ii:ii + 1, lanes], (PEER_NKEYS, LANES)).astype(BF16)
                term = a * jnp.where(br_scr[1, h, :, lanes] < n, br_scr[0, h, :, lanes], 0)
                gate = term if gate is None else gate + term
            wt_scr[rows_, lanes] = gate * jax.nn.gelu(act_scr[rows_, lanes]).astype(BF16)

    def down(c):
        cols = slice(c * PEER_CHUNK, (c + 1) * PEER_CHUNK)
        acc_scr[...] += jnp.dot(vt_ref[:, cols], wt_scr[cols, :], preferred_element_type=F32)

    n_up = 4 * PEER_STEP_CHUNKS
    up(0)
    for q in range(n_up):
        if q + 1 < n_up:
            up(q + 1)
        gate_key(2 * q)
        gate_key(2 * q + 1)
        if q % 4 == 3:
            down(q // 4)

    @pl.when(e == pl.num_programs(1) - 1)
    def _():
        x2 = _layernorm(ALPHA * x_ref[...] + acc_scr[...].T, lng_ref[...], lnb_ref[...])
        o_ref[...] = x2 + _mm(p_ref[...], wp_ref[...]) * jax.nn.sigmoid(_mm(x2, wg_ref[...]))


def _peer_dense(x, an, br, u16, vt16, p, lng, lnb, wp, wg, tq):
    rows = x.shape[0]
    step = PEER_STEP_CHUNKS * PEER_CHUNK
    return pl.pallas_call(
        functools.partial(_peer_dense_kernel, tq=tq),
        grid=(rows // tq, PEER_EXPERTS // step),
        in_specs=[pl.BlockSpec((tq, D_MODEL), lambda t, e: (t, 0)),
                  pl.BlockSpec((2, PEER_HEADS, 8 * PEER_STEP_CHUNKS, tq), lambda t, e: (0, 0, e, t)),
                  pl.BlockSpec((2, PEER_HEADS, PEER_NKEYS, tq), lambda t, e: (0, 0, 0, t)),
                  pl.BlockSpec((step, D_MODEL), lambda t, e: (e, 0)),
                  pl.BlockSpec((D_MODEL, step), lambda t, e: (0, e)),
                  pl.BlockSpec((tq, PLE_DIM), lambda t, e: (t, 0)),
                  _const_spec(lng.shape), _const_spec(lnb.shape), _const_spec(wp.shape), _const_spec(wg.shape)],
        out_specs=pl.BlockSpec((tq, D_MODEL), lambda t, e: (t, 0)),
        out_shape=jax.ShapeDtypeStruct((rows, D_MODEL), F32),
        scratch_shapes=[pltpu.VMEM((D_MODEL, tq), BF16),
                        pltpu.VMEM((D_MODEL, tq), F32),
                        pltpu.VMEM((2, PEER_HEADS, PEER_NKEYS, tq), BF16),
                        pltpu.VMEM((step, tq), F32),
                        pltpu.VMEM((step, tq), BF16)],
        compiler_params=_cparams("arbitrary", "arbitrary"),
        name="peer_dense_ln_ple",
    )(x, an, br, u16, vt16, p, lng, lnb, wp, wg)


def _token_tile(rows, want):
    return math.gcd(rows, want)


def _trunk(x, p, s5, rg, ln, peer, ple):
    batch, seq, _ = x.shape
    rows = batch * seq
    ln1_g, ln1_b, ln2_g, ln2_b = ln
    peer_w_q, peer_subkeys, peer_u, peer_v = peer
    ple_w_proj, ple_w_gate = ple
    xt = jnp.transpose(x, (1, 0, 2)).reshape(rows, D_MODEL)
    pt = jnp.transpose(p, (0, 2, 1, 3)).reshape(DEPTH, rows, PLE_DIM)
    for i in range(DEPTH):
        j = i // 2
        if i % 2 == 0:
            xt = _s5_layer(xt, batch, *(w[j] for w in s5), ln1_g[i], ln1_b[i])
        else:
            xt = _rg_layer(xt, batch, *(w[j] for w in rg), ln1_g[i], ln1_b[i])
        an, br = _peer_topk(xt, peer_w_q[i].T.astype(BF16), peer_subkeys[i].astype(BF16), _token_tile(rows, 256))
        xt = _peer_dense(xt, an, br, peer_u[i].astype(BF16), peer_v[i].T.astype(BF16), pt[i],
                         ln2_g[i][None, :], ln2_b[i][None, :], ple_w_proj[i].astype(BF16),
                         ple_w_gate[i].astype(BF16), _token_tile(rows, 512))
    return jnp.transpose(xt.reshape(seq, batch, D_MODEL), (1, 0, 2))


def kernel(x_prompt, x_sample, p_prompt, p_sample, s5_w_in, s5_lam_re, s5_lam_im, s5_log_step, s5_b_re, s5_b_im, s5_c_re, s5_c_im, s5_d, s5_w_glu, rg_w_in, rg_conv_w, rg_conv_b, rg_w_gate_a, rg_b_gate_a, rg_w_gate_x, rg_b_gate_x, rg_lambda, rg_w_out, ln1_g, ln1_b, ln2_g, ln2_b, peer_w_q, peer_subkeys, peer_u, peer_v, ple_w_proj, ple_w_gate):
    s5 = (s5_w_in, s5_lam_re, s5_lam_im, s5_log_step, s5_b_re, s5_b_im, s5_c_re, s5_c_im, s5_d, s5_w_glu)
    rg = (rg_w_in, rg_conv_w, rg_conv_b, rg_w_gate_a, rg_b_gate_a, rg_w_gate_x, rg_b_gate_x, rg_lambda, rg_w_out)
    ln = (ln1_g, ln1_b, ln2_g, ln2_b)
    peer = (peer_w_q, peer_subkeys, peer_u, peer_v)
    ple = (ple_w_proj, ple_w_gate)
    y_prompt = _trunk(x_prompt, p_prompt, s5, rg, ln, peer, ple)
    y_sample = _trunk(x_sample, p_sample, s5, rg, ln, peer, ple)
    return (y_prompt, y_sample)
```

```python
import functools
import math

import jax
import jax.numpy as jnp
from jax import lax
from jax.experimental import pallas as pl
from jax.experimental.pallas import tpu as pltpu

F32 = jnp.float32
BF16 = jnp.bfloat16

D_MODEL = 1024
DEPTH = 2
S5_GROUP = 16
S5_GROUPS = 64
S5_STATE = 64
S5_SLABS = 4
S5_SLAB_CH = D_MODEL // S5_SLABS
S5_SLAB_ST = 16 * S5_STATE
RG_BLOCKS = 4
RG_BLOCK = 256
RG_CONV = 4
RG_C = 8.0
PEER_HEADS = 8
PEER_NKEYS = 128
PEER_EXPERTS = PEER_NKEYS * PEER_NKEYS
PEER_HALF = 128
PEER_TOPK = 16
PLE_DIM = 256
ALPHA = (2 * DEPTH) ** 0.25
LN_EPS = 1e-5

LANES = 128
VMEM_LIMIT = 56 * 1024 * 1024
MIX_ROWS = 512
RG_ROWS = 1024
SCAN_COLS = 512

_PAIR_COUNTS = [min(PEER_TOPK, PEER_TOPK // (a + 1)) for a in range(PEER_TOPK)]
_PAIR_OFFS = [sum(_PAIR_COUNTS[:a]) for a in range(PEER_TOPK)]
_N_PAIRS = sum(_PAIR_COUNTS)
_PAIR_ROWS = ((_N_PAIRS + 7) // 8) * 8


def _mm(a, b):
    return jnp.dot(a.astype(BF16), b.astype(BF16), preferred_element_type=F32)


def _gelu(x):
    k = math.sqrt(2.0 / math.pi)
    hx = 0.5 * x
    return hx + hx * jnp.tanh(x * (k + (k * 0.044715) * (x * x)))


def _layernorm(z, g, b):
    mu = jnp.mean(z, axis=-1, keepdims=True)
    zc = z - mu
    var = jnp.mean(zc * zc, axis=-1, keepdims=True)
    return zc * lax.rsqrt(var + LN_EPS) * g + b


def _cparams(*sem):
    return pltpu.CompilerParams(dimension_semantics=sem, vmem_limit_bytes=VMEM_LIMIT)


def _const_spec(shape):
    nd = len(shape)
    return pl.BlockSpec(shape, lambda *_: (0,) * nd)


def _s5_kernel(*refs, batch, steps, reverse, final):
    if final:
        (x_ref, win_ref, wb_ref, wc_ref, a_ref, yprev_ref, d_ref, wglu_ref, lng_ref, lnb_ref,
         out_ref, bu_scr, h_scr) = refs
    else:
        x_ref, win_ref, wb_ref, wc_ref, a_ref, out_ref, bu_scr, h_scr = refs

    @pl.when(pl.program_id(0) == 0)
    def _():
        h_scr[...] = jnp.zeros_like(h_scr)

    x = x_ref[...]
    u = _mm(x, win_ref[...])
    ub = u.astype(BF16)
    for s in range(S5_SLABS):
        bu_scr[:, s * 2 * S5_SLAB_ST:(s + 1) * 2 * S5_SLAB_ST] = jnp.dot(
            ub[:, s * S5_SLAB_CH:(s + 1) * S5_SLAB_CH], wb_ref[s], preferred_element_type=F32)

    for s in range(S5_SLABS):
        for cb in range(S5_SLAB_ST // SCAN_COLS):
            c_re = s * 2 * S5_SLAB_ST + cb * SCAN_COLS
            c_im = c_re + S5_SLAB_ST
            a_re = jnp.broadcast_to(a_ref[s:s + 1, cb * SCAN_COLS:(cb + 1) * SCAN_COLS], (batch, SCAN_COLS))
            a_im = jnp.broadcast_to(
                a_ref[S5_SLABS + s:S5_SLABS + s + 1, cb * SCAN_COLS:(cb + 1) * SCAN_COLS], (batch, SCAN_COLS))

            def step(k, carry, c_re=c_re, c_im=c_im, a_re=a_re, a_im=a_im):
                h_re, h_im = carry
                t = (steps - 1 - k) if reverse else k
                row = pl.multiple_of(t * batch, batch)
                b_re = bu_scr[pl.ds(row, batch), c_re:c_re + SCAN_COLS]
                b_im = bu_scr[pl.ds(row, batch), c_im:c_im + SCAN_COLS]
                n_re = a_re * h_re - a_im * h_im + b_re
                n_im = a_re * h_im + a_im * h_re + b_im
                bu_scr[pl.ds(row, batch), c_re:c_re + SCAN_COLS] = n_re
                bu_scr[pl.ds(row, batch), c_im:c_im + SCAN_COLS] = n_im
                return n_re, n_im

            h_re, h_im = lax.fori_loop(
                0, steps, step,
                (h_scr[:, c_re:c_re + SCAN_COLS], h_scr[:, c_im:c_im + SCAN_COLS]), unroll=4)
            h_scr[:, c_re:c_re + SCAN_COLS] = h_re
            h_scr[:, c_im:c_im + SCAN_COLS] = h_im

    ys = []
    for s in range(S5_SLABS):
        hb = bu_scr[:, s * 2 * S5_SLAB_ST:(s + 1) * 2 * S5_SLAB_ST].astype(BF16)
        ys.append(jnp.dot(hb, wc_ref[s], preferred_element_type=F32))
    y = jnp.concatenate(ys, axis=1)

    if not final:
        out_ref[...] = y
        return

    y = y + yprev_ref[...] + d_ref[...] * u
    hg = _gelu(y)
    vg = _mm(hg, wglu_ref[...])
    mix = vg[:, :D_MODEL] * jax.nn.sigmoid(vg[:, D_MODEL:])
    out_ref[...] = _layernorm(ALPHA * x + mix, lng_ref[...], lnb_ref[...])


def _s5_discretize(lam_re, lam_im, log_step, b_re, b_im):
    step = jnp.exp(log_step.astype(F32))[:, None]
    lr = lam_re.astype(F32)
    li = lam_im.astype(F32)
    mag = jnp.exp(lr * step)
    ang = li * step
    ar = mag * jnp.cos(ang)
    ai = mag * jnp.sin(ang)
    den = lr * lr + li * li
    zr = ar - 1.0
    qr = (zr * lr + ai * li) / den
    qi = (ai * lr - zr * li) / den
    br = b_re.astype(F32)
    bi = b_im.astype(F32)
    bbr = qr[..., None] * br - qi[..., None] * bi
    bbi = qr[..., None] * bi + qi[..., None] * br
    return ar, ai, bbr, bbi


def _s5_weights(lam_re, lam_im, log_step, b_re, b_im, c_re, c_im):
    ar, ai, bbr, bbi = _s5_discretize(lam_re, lam_im, log_step, b_re, b_im)
    eye = jnp.eye(16, dtype=F32)

    def in_blocks(bb):
        bb4 = bb.reshape(S5_SLABS, 16, S5_STATE, S5_GROUP)
        return jnp.einsum('sgpc,gh->sgchp', bb4, eye).reshape(S5_SLABS, S5_SLAB_CH, S5_SLAB_ST)

    def out_blocks(cc):
        cc4 = cc.astype(F32).reshape(S5_SLABS, 16, S5_GROUP, S5_STATE)
        return jnp.einsum('sgcp,gh->sgphc', cc4, eye).reshape(S5_SLABS, S5_SLAB_ST, S5_SLAB_CH)

    wb = jnp.concatenate([in_blocks(bbr), in_blocks(bbi)], axis=2).astype(BF16)
    wc = jnp.concatenate([out_blocks(c_re), -out_blocks(c_im)], axis=1).astype(BF16)
    avec = jnp.concatenate([ar.reshape(S5_SLABS, S5_SLAB_ST), ai.reshape(S5_SLABS, S5_SLAB_ST)], axis=0)
    return wb, wc, avec


def _s5_call(x, win, wb, wc, avec, batch, reverse, tail):
    rows = x.shape[0]
    n = rows // MIX_ROWS
    steps = MIX_ROWS // batch
    final = tail is not None
    if reverse:
        tok = pl.BlockSpec((MIX_ROWS, D_MODEL), lambda i: (n - 1 - i, 0))
    else:
        tok = pl.BlockSpec((MIX_ROWS, D_MODEL), lambda i: (i, 0))
    args = [x, win, wb, wc, avec]
    specs = [tok, _const_spec(win.shape), _const_spec(wb.shape), _const_spec(wc.shape), _const_spec(avec.shape)]
    if final:
        yprev, dskip, wglu, lng, lnb = tail
        args += [yprev, dskip, wglu, lng, lnb]
        specs += [tok, _const_spec(dskip.shape), _const_spec(wglu.shape), _const_spec(lng.shape),
                  _const_spec(lnb.shape)]
    return pl.pallas_call(
        functools.partial(_s5_kernel, batch=batch, steps=steps, reverse=reverse, final=final),
        grid=(n,),
        in_specs=specs,
        out_specs=tok,
        out_shape=jax.ShapeDtypeStruct((rows, D_MODEL), F32),
        scratch_shapes=[pltpu.VMEM((MIX_ROWS, S5_SLABS * 2 * S5_SLAB_ST), F32),
                        pltpu.VMEM((batch, S5_SLABS * 2 * S5_SLAB_ST), F32)],
        compiler_params=_cparams("arbitrary"),
        name="s5_bwd_glu_ln" if final else "s5_fwd",
    )(*args)


def _s5_layer(x, batch, w_in, lam_re, lam_im, log_step, b_re, b_im, c_re, c_im, d_skip, w_glu, ln_g, ln_b):
    win = w_in.astype(BF16)
    wb0, wc0, av0 = _s5_weights(lam_re[0], lam_im[0], log_step[0], b_re[0], b_im[0], c_re[0], c_im[0])
    wb1, wc1, av1 = _s5_weights(lam_re[1], lam_im[1], log_step[1], b_re[1], b_im[1], c_re[1], c_im[1])
    y_fwd = _s5_call(x, win, wb0, wc0, av0, batch, False, None)
    tail = (y_fwd, d_skip.astype(F32)[None, :], w_glu.astype(BF16), ln_g[None, :], ln_b[None, :])
    return _s5_call(x, win, wb1, wc1, av1, batch, True, tail)


def _rg_kernel(*refs, batch, steps, reverse, final, n_chunks):
    if final:
        (x_ref, xp_ref, xn_ref, wr_ref, cw_ref, cb_ref, wa_ref, ba_ref, wx_ref, bx_ref, sp_ref,
         hprev_ref, wg_ref, wo_ref, lng_ref, lnb_ref, out_ref, xe_scr, a_scr, b_scr, h_scr) = refs
    else:
        (x_ref, xp_ref, xn_ref, wr_ref, cw_ref, cb_ref, wa_ref, ba_ref, wx_ref, bx_ref, sp_ref,
         out_ref, xe_scr, a_scr, b_scr, h_scr) = refs
    rows = steps * batch
    i = pl.program_id(0)
    chunk = (n_chunks - 1 - i) if reverse else i

    @pl.when(i == 0)
    def _():
        h_scr[...] = jnp.zeros_like(h_scr)

    x = x_ref[...]
    keep_prev = (chunk > 0).astype(F32)
    keep_next = (chunk < n_chunks - 1).astype(F32)
    xe_scr[0:batch, :] = xp_ref[...] * keep_prev
    xe_scr[batch:batch + rows, :] = x
    xe_scr[batch + rows:, :] = xn_ref[...] * keep_next
    r_ext = _mm(xe_scr[...], wr_ref[...])
    xe_scr[...] = r_ext

    c = cb_ref[...] + xe_scr[0:rows, :] * cw_ref[0:1, :]
    for k in range(1, RG_CONV):
        c = c + xe_scr[k * batch:k * batch + rows, :] * cw_ref[k:k + 1, :]

    cb16 = c.astype(BF16)
    ga = []
    gx = []
    for blk in range(RG_BLOCKS):
        cs = cb16[:, blk * RG_BLOCK:(blk + 1) * RG_BLOCK]
        ga.append(jnp.dot(cs, wa_ref[blk], preferred_element_type=F32))
        gx.append(jnp.dot(cs, wx_ref[blk], preferred_element_type=F32))
    r_gate = jax.nn.sigmoid(jnp.concatenate(ga, axis=1) + ba_ref[...])
    i_gate = jax.nn.sigmoid(jnp.concatenate(gx, axis=1) + bx_ref[...])
    log_a = -RG_C * r_gate * sp_ref[...]
    a_scr[...] = jnp.exp(log_a)
    th = jnp.tanh(log_a)
    b_scr[...] = jnp.sqrt(-2.0 * th / (1.0 - th)) * (i_gate * c)

    def step(k, h):
        t = (steps - 1 - k) if reverse else k
        row = pl.multiple_of(t * batch, batch)
        h = a_scr[pl.ds(row, batch), :] * h + b_scr[pl.ds(row, batch), :]
        b_scr[pl.ds(row, batch), :] = h
        return h

    h_scr[...] = lax.fori_loop(0, steps, step, h_scr[...], unroll=4)

    if not final:
        out_ref[...] = b_scr[...]
        return

    h_tot = b_scr[...] + hprev_ref[...]
    g = _mm(x, wg_ref[...])
    y = h_tot * _gelu(g)
    mix = _mm(y, wo_ref[...])
    out_ref[...] = _layernorm(ALPHA * x + mix, lng_ref[...], lnb_ref[...])


def _rg_call(x, wr, cw, cb, wa, ba, wx, bx, sp, batch, reverse, tail):
    rows = x.shape[0]
    chunk = math.gcd(rows, RG_ROWS)
    n = rows // chunk
    steps = chunk // batch
    final = tail is not None
    per_prev = chunk // batch
    per_next = chunk // (2 * batch)
    last_next = rows // (2 * batch) - 1

    def cidx(i):
        return (n - 1 - i) if reverse else i

    tok = pl.BlockSpec((chunk, D_MODEL), lambda i: (cidx(i), 0))
    prev = pl.BlockSpec((batch, D_MODEL), lambda i: (jnp.maximum(cidx(i) * per_prev - 1, 0), 0))
    nxt = pl.BlockSpec((2 * batch, D_MODEL), lambda i: (jnp.minimum((cidx(i) + 1) * per_next, last_next), 0))
    args = [x, x, x, wr, cw, cb, wa, ba, wx, bx, sp]
    specs = [tok, prev, nxt] + [_const_spec(a.shape) for a in args[3:]]
    if final:
        args += list(tail)
        specs += [tok] + [_const_spec(a.shape) for a in tail[1:]]
    return pl.pallas_call(
        functools.partial(_rg_kernel, batch=batch, steps=steps, reverse=reverse, final=final, n_chunks=n),
        grid=(n,),
        in_specs=specs,
        out_specs=tok,
        out_shape=jax.ShapeDtypeStruct((rows, D_MODEL), F32),
        scratch_shapes=[pltpu.VMEM((chunk + 3 * batch, D_MODEL), F32),
                        pltpu.VMEM((chunk, D_MODEL), F32),
                        pltpu.VMEM((chunk, D_MODEL), F32),
                        pltpu.VMEM((batch, D_MODEL), F32)],
        compiler_params=_cparams("arbitrary"),
        name="rg_bwd_out_ln" if final else "rg_fwd",
    )(*args)


def _rg_layer(x, batch, w_in, conv_w, conv_b, w_ga, b_ga, w_gx, b_gx, lam, w_out, ln_g, ln_b):
    wg = w_in[:, :D_MODEL].astype(BF16)
    wr = w_in[:, D_MODEL:].astype(BF16)
    cw = jnp.concatenate([conv_w.astype(F32), jnp.zeros((8 - RG_CONV, D_MODEL), F32)], axis=0)
    cb = conv_b.astype(F32)[None, :]
    sp = jax.nn.softplus(-lam.astype(F32))
    common = lambda d: (wr, cw, cb, w_ga[d].astype(BF16), b_ga[d][None, :].astype(F32),
                        w_gx[d].astype(BF16), b_gx[d][None, :].astype(F32), sp[d][None, :])
    h_fwd = _rg_call(x, *common(0), batch, False, None)
    tail = (h_fwd, wg, w_out.astype(BF16), ln_g[None, :], ln_b[None, :])
    return _rg_call(x, *common(1), batch, True, tail)


def _sorting_pairs(n):
    pairs = []
    p = 1
    while p < n:
        k = p
        while k >= 1:
            for j in range(k % p, n - k, 2 * k):
                for i in range(min(k, n - j - k)):
                    if (i + j) // (2 * p) == (i + j + k) // (2 * p):
                        pairs.append((i + j, i + j + k))
            k //= 2
        p *= 2
    return pairs


_SORT16 = _sorting_pairs(16)
_SORT8 = _sorting_pairs(8)


def _exchange(x, i, j):
    x[i], x[j] = jnp.maximum(x[i], x[j]), jnp.minimum(x[i], x[j])


def _bitonic_merge(x):
    d = len(x) // 2
    while d >= 1:
        for i in range(len(x)):
            if i & d == 0:
                _exchange(x, i, i + d)
        d //= 2


def _merge_top16_across_sublanes(x, shifts):
    for shift in shifts:
        y = [pltpu.roll(v, shift, 0) for v in x]
        x = [jnp.maximum(x[k], y[PEER_TOPK - 1 - k]) for k in range(PEER_TOPK)]
        _bitonic_merge(x)
    return x


def _count_ge(tiles, thr):
    cnt = None
    for t in tiles:
        c = jnp.where(t >= thr, 1.0, 0.0)
        cnt = c if cnt is None else cnt + c
    for shift in (4, 2, 1):
        cnt = cnt + pltpu.roll(cnt, shift, 0)
    return cnt


def _peer_topk_fast(sk_ref, an_ref, br_ref, qt_scr, tq):
    neg_inf = jnp.float32(-jnp.inf)
    sub_iota = lax.broadcasted_iota(jnp.int32, (8, LANES), 0)
    sub_is = [sub_iota == r for r in range(8)]

    def head_body(h, bad):
        for lt in range(tq // LANES):
            lanes = slice(lt * LANES, (lt + 1) * LANES)
            tiles = []
            tops = []
            for c in range(2):
                row0 = pl.multiple_of((2 * h + c) * PEER_HALF, PEER_HALF)
                q = qt_scr[pl.ds(row0, PEER_HALF), lanes]
                s = jnp.dot(sk_ref[c], q.astype(BF16), preferred_element_type=F32)
                t = [s[v * 8:(v + 1) * 8, :] for v in range(PEER_NKEYS // 8)]
                x = list(t)
                for i, j in _SORT16:
                    _exchange(x, i, j)
                tiles.append(t)
                tops.append(_merge_top16_across_sublanes(x, (4, 2, 1)))
            v0, v1 = tops

            cand = [[v0[a] + v1[b] for b in range(_PAIR_COUNTS[a])] for a in range(PEER_TOPK)]
            flat = [cand[a][b] for a in range(PEER_TOPK) for b in range(_PAIR_COUNTS[a])]
            packed = []
            for v in range(_PAIR_ROWS // 8):
                p = jnp.full((8, LANES), neg_inf, F32)
                for r in range(8):
                    if v * 8 + r < _N_PAIRS:
                        p = jnp.where(sub_is[r], flat[v * 8 + r], p)
                packed.append(p)
            x = packed + [jnp.full((8, LANES), neg_inf, F32)]
            for i, j in _SORT8:
                _exchange(x, i, j)
            y = [pltpu.roll(v, 4, 0) for v in x]
            x = x + y[::-1]
            _bitonic_merge(x)
            x = _merge_top16_across_sublanes(x, (2, 1))
            tau = x[PEER_TOPK - 1]
            bad = bad + jnp.where(_count_ge(packed, tau) != float(PEER_TOPK), 1.0, 0.0)

            top = flat[0]
            z = None
            for p in packed:
                e = jnp.where(p >= tau, jnp.exp(p - top), 0.0)
                z = e if z is None else z + e
            for shift in (4, 2, 1):
                z = z + pltpu.roll(z, shift, 0)
            inv_z = 1.0 / z

            counts = []
            for a in range(PEER_TOPK):
                n_a = None
                for b in range(_PAIR_COUNTS[a]):
                    c = jnp.where(cand[a][b] >= tau, 1.0, 0.0)
                    n_a = c if n_a is None else n_a + c
                counts.append(n_a)

            over0 = _count_ge(tiles[0], v0[PEER_TOPK - 1]) > float(PEER_TOPK)
            over1 = _count_ge(tiles[1], v1[PEER_TOPK - 1]) > float(PEER_TOPK)
            bad = bad + jnp.where(over0, counts[PEER_TOPK - 1], 0.0)
            for a in range(PEER_TOPK - 1):
                bad = bad + jnp.where(v0[a] == v0[a + 1], jnp.abs(counts[a] - counts[a + 1]), 0.0)
            for a in range(PEER_TOPK):
                edge = _PAIR_COUNTS[a]
                full = counts[a] == float(edge)
                if edge < PEER_TOPK:
                    tie = v1[edge - 1] == v1[edge]
                else:
                    tie = over1
                bad = bad + jnp.where(tie, jnp.where(full, 1.0, 0.0), 0.0)

            for v in range(PEER_NKEYS // 8):
                rows_ = slice(v * 8, (v + 1) * 8)
                s0 = tiles[0][v]
                s1 = tiles[1][v]
                n_dense = jnp.zeros((8, LANES), F32)
                rank1 = jnp.zeros((8, LANES), F32)
                for a in range(PEER_TOPK):
                    n_dense = jnp.where(s0 == v0[a], counts[a], n_dense)
                    rank1 = rank1 + jnp.where(v1[a] > s1, 1.0, 0.0)
                an_ref[0, h, rows_, lanes] = jnp.exp(s0 - v0[0]) * inv_z
                an_ref[1, h, rows_, lanes] = n_dense
                br_ref[0, h, rows_, lanes] = jnp.exp(s1 - v1[0]).astype(BF16)
                br_ref[1, h, rows_, lanes] = rank1.astype(BF16)
        return bad

    return lax.fori_loop(0, PEER_HEADS, head_body, jnp.zeros((8, LANES), F32))


def _peer_topk_exact(sk_ref, an_ref, br_ref, qt_scr, vals_scr, rank_scr, e_scr, cand_scr, sel_scr, tq):
    neg_inf = jnp.float32(-jnp.inf)
    n_lt = tq // LANES
    iota_k = lax.broadcasted_iota(jnp.int32, (PEER_NKEYS, LANES), 0).astype(F32)

    def half_body(hc, _):
        q = qt_scr[pl.ds(pl.multiple_of(hc * PEER_HALF, PEER_HALF), PEER_HALF), :]
        s = jnp.dot(sk_ref[hc % 2], q.astype(BF16), preferred_element_type=F32)
        for lt in range(n_lt):
            lanes = slice(lt * LANES, (lt + 1) * LANES)
            s_l = s[:, lanes]

            work = s_l
            rank = jnp.full((PEER_NKEYS, LANES), float(PEER_NKEYS), F32)
            for r in range(PEER_TOPK):
                m = jnp.max(work, axis=0, keepdims=True)
                pos = jnp.min(jnp.where(work == m, iota_k, float(PEER_NKEYS)), axis=0, keepdims=True)
                hit = iota_k == pos
                vals_scr[hc, r:r + 1, lanes] = m
                work = jnp.where(hit, neg_inf, work)
                rank = jnp.where(hit, float(r), rank)
            top = vals_scr[hc, 0:1, lanes]
            rank_scr[hc, :, lanes] = rank
            e_scr[hc, :, lanes] = jnp.where(rank < PEER_TOPK, jnp.exp(s_l - top), 0.0)
        return 0

    lax.fori_loop(0, 2 * PEER_HEADS, half_body, 0)

    iota_p = lax.broadcasted_iota(jnp.int32, (_PAIR_ROWS, LANES), 0).astype(F32)

    def head_body(h, _):
        for lt in range(n_lt):
            lanes = slice(lt * LANES, (lt + 1) * LANES)
            v0 = vals_scr[2 * h, :, lanes]
            v1 = vals_scr[2 * h + 1, :, lanes]
            cand_scr[...] = jnp.full((_PAIR_ROWS, LANES), neg_inf, F32)
            for a in range(PEER_TOPK):
                cand_scr[_PAIR_OFFS[a]:_PAIR_OFFS[a] + _PAIR_COUNTS[a], :] = (
                    v0[a:a + 1, :] + v1[0:_PAIR_COUNTS[a], :])
            cand = cand_scr[...]

            def extract(r, carry):
                work, sel = carry
                m = jnp.max(work, axis=0, keepdims=True)
                pos = jnp.min(jnp.where(work == m, iota_p, float(_PAIR_ROWS)), axis=0, keepdims=True)
                hit = iota_p == pos
                return jnp.where(hit, neg_inf, work), jnp.where(hit, 1.0, sel)

            _, sel = lax.fori_loop(0, PEER_TOPK, extract, (cand, jnp.zeros((_PAIR_ROWS, LANES), F32)))
            sel_scr[...] = sel
            top = cand_scr[0:1, :]
            z = jnp.sum(jnp.where(sel > 0.0, jnp.exp(cand - top), 0.0), axis=0, keepdims=True)
            inv_z = 1.0 / z
            rank0 = rank_scr[2 * h, :, lanes]
            n_dense = jnp.zeros((PEER_NKEYS, LANES), F32)
            for a in range(PEER_TOPK):
                n_a = jnp.sum(sel_scr[_PAIR_OFFS[a]:_PAIR_OFFS[a] + _PAIR_COUNTS[a], :], axis=0, keepdims=True)
                n_dense = jnp.where(rank0 == float(a), n_a, n_dense)
            an_ref[0, h, :, lanes] = e_scr[2 * h, :, lanes] * inv_z
            an_ref[1, h, :, lanes] = n_dense
            br_ref[0, h, :, lanes] = e_scr[2 * h + 1, :, lanes].astype(BF16)
            br_ref[1, h, :, lanes] = rank_scr[2 * h + 1, :, lanes].astype(BF16)
        return 0

    lax.fori_loop(0, PEER_HEADS, head_body, 0)


def _peer_topk_kernel(x_ref, wqt_ref, sk_ref, an_ref, br_ref, qt_scr, vals_scr, rank_scr, e_scr, cand_scr,
                      sel_scr, *, tq):
    qt_scr[...] = lax.dot_general(wqt_ref[...], x_ref[...].astype(BF16), (((1,), (1,)), ((), ())),
                                  preferred_element_type=F32)
    bad = _peer_topk_fast(sk_ref, an_ref, br_ref, qt_scr, tq)

    @pl.when(jnp.max(bad) > 0.0)
    def _():
        _peer_topk_exact(sk_ref, an_ref, br_ref, qt_scr, vals_scr, rank_scr, e_scr, cand_scr, sel_scr, tq)


def _peer_topk(x, wqt, sk, tq):
    rows = x.shape[0]
    return pl.pallas_call(
        functools.partial(_peer_topk_kernel, tq=tq),
        grid=(rows // tq,),
        in_specs=[pl.BlockSpec((tq, D_MODEL), lambda i: (i, 0)), _const_spec(wqt.shape), _const_spec(sk.shape)],
        out_specs=[pl.BlockSpec((2, PEER_HEADS, PEER_NKEYS, tq), lambda i: (0, 0, 0, i)),
                   pl.BlockSpec((2, PEER_HEADS, PEER_NKEYS, tq), lambda i: (0, 0, 0, i))],
        out_shape=[jax.ShapeDtypeStruct((2, PEER_HEADS, PEER_NKEYS, rows), F32),
                   jax.ShapeDtypeStruct((2, PEER_HEADS, PEER_NKEYS, rows), BF16)],
        scratch_shapes=[pltpu.VMEM((2 * PEER_HEADS * PEER_HALF, tq), F32),
                        pltpu.VMEM((2 * PEER_HEADS, PEER_TOPK, tq), F32),
                        pltpu.VMEM((2 * PEER_HEADS, PEER_NKEYS, tq), F32),
                        pltpu.VMEM((2 * PEER_HEADS, PEER_NKEYS, tq), F32),
                        pltpu.VMEM((_PAIR_ROWS, LANES), F32),
                        pltpu.VMEM((_PAIR_ROWS, LANES), F32)],
        compiler_params=_cparams("arbitrary"),
        name="peer_topk",
    )(x, wqt, sk)


PEER_CHUNK = 8 * PEER_NKEYS


PEER_STEP_CHUNKS = 2


def _peer_dense_kernel(x_ref, an_ref, br_ref, u_ref, vt_ref, p_ref, lng_ref, lnb_ref, wp_ref, wg_ref, o_ref,
                       xt_scr, acc_scr, br_scr, act_scr, wt_scr, *, tq):
    e = pl.program_id(1)

    @pl.when(e == 0)
    def _():
        xt_scr[...] = x_ref[...].T.astype(BF16)
        acc_scr[...] = jnp.zeros_like(acc_scr)
        br_scr[...] = br_ref[...]

    def up(q):
        rows_ = slice(2 * q * PEER_NKEYS, 2 * (q + 1) * PEER_NKEYS)
        act_scr[rows_, :] = jnp.dot(u_ref[rows_, :], xt_scr[...], preferred_element_type=F32)

    def gate_key(ii):
        rows_ = slice(ii * PEER_NKEYS, (ii + 1) * PEER_NKEYS)
        for lt in range(tq // LANES):
            lanes = slice(lt * LANES, (lt + 1) * LANES)
            gate = None
            for h in range(PEER_HEADS):
                a = jnp.broadcast_to(an_ref[0, h, ii:ii + 1, lanes], (PEER_NKEYS, LANES)).astype(BF16)
                n = jnp.broadcast_to(an_ref[1, h, ii:ii + 1, lanes], (PEER_NKEYS, LANES)).astype(BF16)
                term = a * jnp.where(br_scr[1, h, :, lanes] < n, br_scr[0, h, :, lanes], 0)
                gate = term if gate is None else gate + term
            wt_scr[rows_, lanes] = gate * _gelu(act_scr[rows_, lanes]).astype(BF16)

    def down(c):
        cols = slice(c * PEER_CHUNK, (c + 1) * PEER_CHUNK)
        acc_scr[...] += jnp.dot(vt_ref[:, cols], wt_scr[cols, :], preferred_element_type=F32)

    n_up = 4 * PEER_STEP_CHUNKS
    up(0)
    for q in range(n_up):
        if q + 1 < n_up:
            up(q + 1)
        gate_key(2 * q)
        gate_key(2 * q + 1)
        if q % 4 == 3:
            down(q // 4)

    @pl.when(e == pl.num_programs(1) - 1)
    def _():
        x2 = _layernorm(ALPHA * x_ref[...] + acc_scr[...].T, lng_ref[...], lnb_ref[...])
        o_ref[...] = x2 + _mm(p_ref[...], wp_ref[...]) * jax.nn.sigmoid(_mm(x2, wg_ref[...]))


def _peer_dense(x, an, br, u16, vt16, p, lng, lnb, wp, wg, tq):
    rows = x.shape[0]
    step = PEER_STEP_CHUNKS * PEER_CHUNK
    return pl.pallas_call(
        functools.partial(_peer_dense_kernel, tq=tq),
        grid=(rows // tq, PEER_EXPERTS // step),
        in_specs=[pl.BlockSpec((tq, D_MODEL), lambda t, e: (t, 0)),
                  pl.BlockSpec((2, PEER_HEADS, 8 * PEER_STEP_CHUNKS, tq), lambda t, e: (0, 0, e, t)),
                  pl.BlockSpec((2, PEER_HEADS, PEER_NKEYS, tq), lambda t, e: (0, 0, 0, t)),
                  pl.BlockSpec((step, D_MODEL), lambda t, e: (e, 0)),
                  pl.BlockSpec((D_MODEL, step), lambda t, e: (0, e)),
                  pl.BlockSpec((tq, PLE_DIM), lambda t, e: (t, 0)),
                  _const_spec(lng.shape), _const_spec(lnb.shape), _const_spec(wp.shape), _const_spec(wg.shape)],
        out_specs=pl.BlockSpec((tq, D_MODEL), lambda t, e: (t, 0)),
        out_shape=jax.ShapeDtypeStruct((rows, D_MODEL), F32),
        scratch_shapes=[pltpu.VMEM((D_MODEL, tq), BF16),
                        pltpu.VMEM((D_MODEL, tq), F32),
                        pltpu.VMEM((2, PEER_HEADS, PEER_NKEYS, tq), BF16),
                        pltpu.VMEM((step, tq), F32),
                        pltpu.VMEM((step, tq), BF16)],
        compiler_params=_cparams("arbitrary", "arbitrary"),
        name="peer_dense_ln_ple",
    )(x, an, br, u16, vt16, p, lng, lnb, wp, wg)


def _token_tile(rows, want):
    return math.gcd(rows, want)


def _trunk(x, p, s5, rg, ln, peer, ple):
    batch, seq, _ = x.shape
    rows = batch * seq
    ln1_g, ln1_b, ln2_g, ln2_b = ln
    peer_w_q, peer_subkeys, peer_u, peer_v = peer
    ple_w_proj, ple_w_gate = ple
    xt = jnp.transpose(x, (1, 0, 2)).reshape(rows, D_MODEL)
    pt = jnp.transpose(p, (0, 2, 1, 3)).reshape(DEPTH, rows, PLE_DIM)
    for i in range(DEPTH):
        j = i // 2
        if i % 2 == 0:
            xt = _s5_layer(xt, batch, *(w[j] for w in s5), ln1_g[i], ln1_b[i])
        else:
            xt = _rg_layer(xt, batch, *(w[j] for w in rg), ln1_g[i], ln1_b[i])
        an, br = _peer_topk(xt, peer_w_q[i].T.astype(BF16), peer_subkeys[i].astype(BF16), _token_tile(rows, 256))
        xt = _peer_dense(xt, an, br, peer_u[i].astype(BF16), peer_v[i].T.astype(BF16), pt[i],
                         ln2_g[i][None, :], ln2_b[i][None, :], ple_w_proj[i].astype(BF16),
                         ple_w_gate[i].astype(BF16), _token_tile(rows, 512))
    return jnp.transpose(xt.reshape(seq, batch, D_MODEL), (1, 0, 2))


def kernel(x_prompt, x_sample, p_prompt, p_sample, s5_w_in, s5_lam_re, s5_lam_im, s5_log_step, s5_b_re, s5_b_im, s5_c_re, s5_c_im, s5_d, s5_w_glu, rg_w_in, rg_conv_w, rg_conv_b, rg_w_gate_a, rg_b_gate_a, rg_w_gate_x, rg_b_gate_x, rg_lambda, rg_w_out, ln1_g, ln1_b, ln2_g, ln2_b, peer_w_q, peer_subkeys, peer_u, peer_v, ple_w_proj, ple_w_gate):
    s5 = (s5_w_in, s5_lam_re, s5_lam_im, s5_log_step, s5_b_re, s5_b_im, s5_c_re, s5_c_im, s5_d, s5_w_glu)
    rg = (rg_w_in, rg_conv_w, rg_conv_b, rg_w_gate_a, rg_b_gate_a, rg_w_gate_x, rg_b_gate_x, rg_lambda, rg_w_out)
    ln = (ln1_g, ln1_b, ln2_g, ln2_b)
    peer = (peer_w_q, peer_subkeys, peer_u, peer_v)
    ple = (ple_w_proj, ple_w_gate)
    y_prompt = _trunk(x_prompt, p_prompt, s5, rg, ln, peer, ple)
    y_sample = _trunk(x_sample, p_sample, s5, rg, ln, peer, ple)
    return (y_prompt, y_sample)
```

```python
import functools
import math

import jax
import jax.numpy as jnp
from jax import lax
from jax.experimental import pallas as pl
from jax.experimental.pallas import tpu as pltpu

F32 = jnp.float32
BF16 = jnp.bfloat16

D_MODEL = 1024
DEPTH = 2
S5_GROUP = 16
S5_GROUPS = 64
S5_STATE = 64
S5_SLABS = 4
S5_SLAB_CH = D_MODEL // S5_SLABS
S5_SLAB_ST = 16 * S5_STATE
RG_BLOCKS = 4
RG_BLOCK = 256
RG_CONV = 4
RG_C = 8.0
PEER_HEADS = 8
PEER_NKEYS = 128
PEER_EXPERTS = PEER_NKEYS * PEER_NKEYS
PEER_HALF = 128
PEER_TOPK = 16
PLE_DIM = 256
ALPHA = (2 * DEPTH) ** 0.25
LN_EPS = 1e-5

LANES = 128
VMEM_LIMIT = 56 * 1024 * 1024
MIX_ROWS = 512
RG_ROWS = 1024
SCAN_COLS = 512

_PAIR_COUNTS = [min(PEER_TOPK, PEER_TOPK // (a + 1)) for a in range(PEER_TOPK)]
_PAIR_OFFS = [sum(_PAIR_COUNTS[:a]) for a in range(PEER_TOPK)]
_N_PAIRS = sum(_PAIR_COUNTS)
_PAIR_ROWS = ((_N_PAIRS + 7) // 8) * 8


def _mm(a, b):
    return jnp.dot(a.astype(BF16), b.astype(BF16), preferred_element_type=F32)


def _gelu(x):
    k = math.sqrt(2.0 / math.pi)
    hx = 0.5 * x
    return hx + hx * jnp.tanh(x * (k + (k * 0.044715) * (x * x)))


def _layernorm(z, g, b):
    mu = jnp.mean(z, axis=-1, keepdims=True)
    zc = z - mu
    var = jnp.mean(zc * zc, axis=-1, keepdims=True)
    return zc * lax.rsqrt(var + LN_EPS) * g + b


def _cparams(*sem):
    return pltpu.CompilerParams(dimension_semantics=sem, vmem_limit_bytes=VMEM_LIMIT)


def _const_spec(shape):
    nd = len(shape)
    return pl.BlockSpec(shape, lambda *_: (0,) * nd)


def _s5_kernel(*refs, batch, steps, reverse, final):
    if final:
        (x_ref, win_ref, wb_ref, wc_ref, a_ref, yprev_ref, d_ref, wglu_ref, lng_ref, lnb_ref,
         out_ref, bu_scr, h_scr) = refs
    else:
        x_ref, win_ref, wb_ref, wc_ref, a_ref, out_ref, bu_scr, h_scr = refs

    @pl.when(pl.program_id(0) == 0)
    def _():
        h_scr[...] = jnp.zeros_like(h_scr)

    x = x_ref[...]
    u = _mm(x, win_ref[...])
    ub = u.astype(BF16)
    for s in range(S5_SLABS):
        bu_scr[:, s * 2 * S5_SLAB_ST:(s + 1) * 2 * S5_SLAB_ST] = jnp.dot(
            ub[:, s * S5_SLAB_CH:(s + 1) * S5_SLAB_CH], wb_ref[s], preferred_element_type=F32)

    for s in range(S5_SLABS):
        for cb in range(S5_SLAB_ST // SCAN_COLS):
            c_re = s * 2 * S5_SLAB_ST + cb * SCAN_COLS
            c_im = c_re + S5_SLAB_ST
            a_re = jnp.broadcast_to(a_ref[s:s + 1, cb * SCAN_COLS:(cb + 1) * SCAN_COLS], (batch, SCAN_COLS))
            a_im = jnp.broadcast_to(
                a_ref[S5_SLABS + s:S5_SLABS + s + 1, cb * SCAN_COLS:(cb + 1) * SCAN_COLS], (batch, SCAN_COLS))

            def step(k, carry, c_re=c_re, c_im=c_im, a_re=a_re, a_im=a_im):
                h_re, h_im = carry
                t = (steps - 1 - k) if reverse else k
                row = pl.multiple_of(t * batch, batch)
                b_re = bu_scr[pl.ds(row, batch), c_re:c_re + SCAN_COLS]
                b_im = bu_scr[pl.ds(row, batch), c_im:c_im + SCAN_COLS]
                n_re = a_re * h_re - a_im * h_im + b_re
                n_im = a_re * h_im + a_im * h_re + b_im
                bu_scr[pl.ds(row, batch), c_re:c_re + SCAN_COLS] = n_re
                bu_scr[pl.ds(row, batch), c_im:c_im + SCAN_COLS] = n_im
                return n_re, n_im

            h_re, h_im = lax.fori_loop(
                0, steps, step,
                (h_scr[:, c_re:c_re + SCAN_COLS], h_scr[:, c_im:c_im + SCAN_COLS]), unroll=4)
            h_scr[:, c_re:c_re + SCAN_COLS] = h_re
            h_scr[:, c_im:c_im + SCAN_COLS] = h_im

    ys = []
    for s in range(S5_SLABS):
        hb = bu_scr[:, s * 2 * S5_SLAB_ST:(s + 1) * 2 * S5_SLAB_ST].astype(BF16)
        ys.append(jnp.dot(hb, wc_ref[s], preferred_element_type=F32))
    y = jnp.concatenate(ys, axis=1)

    if not final:
        out_ref[...] = y
        return

    y = y + yprev_ref[...] + d_ref[...] * u
    hg = _gelu(y)
    vg = _mm(hg, wglu_ref[...])
    mix = vg[:, :D_MODEL] * jax.nn.sigmoid(vg[:, D_MODEL:])
    out_ref[...] = _layernorm(ALPHA * x + mix, lng_ref[...], lnb_ref[...])


def _s5_discretize(lam_re, lam_im, log_step, b_re, b_im):
    step = jnp.exp(log_step.astype(F32))[:, None]
    lr = lam_re.astype(F32)
    li = lam_im.astype(F32)
    mag = jnp.exp(lr * step)
    ang = li * step
    ar = mag * jnp.cos(ang)
    ai = mag * jnp.sin(ang)
    den = lr * lr + li * li
    zr = ar - 1.0
    qr = (zr * lr + ai * li) / den
    qi = (ai * lr - zr * li) / den
    br = b_re.astype(F32)
    bi = b_im.astype(F32)
    bbr = qr[..., None] * br - qi[..., None] * bi
    bbi = qr[..., None] * bi + qi[..., None] * br
    return ar, ai, bbr, bbi


def _s5_weights(lam_re, lam_im, log_step, b_re, b_im, c_re, c_im):
    ar, ai, bbr, bbi = _s5_discretize(lam_re, lam_im, log_step, b_re, b_im)
    eye = jnp.eye(16, dtype=F32)

    def in_blocks(bb):
        bb4 = bb.reshape(S5_SLABS, 16, S5_STATE, S5_GROUP)
        return jnp.einsum('sgpc,gh->sgchp', bb4, eye).reshape(S5_SLABS, S5_SLAB_CH, S5_SLAB_ST)

    def out_blocks(cc):
        cc4 = cc.astype(F32).reshape(S5_SLABS, 16, S5_GROUP, S5_STATE)
        return jnp.einsum('sgcp,gh->sgphc', cc4, eye).reshape(S5_SLABS, S5_SLAB_ST, S5_SLAB_CH)

    wb = jnp.concatenate([in_blocks(bbr), in_blocks(bbi)], axis=2).astype(BF16)
    wc = jnp.concatenate([out_blocks(c_re), -out_blocks(c_im)], axis=1).astype(BF16)
    avec = jnp.concatenate([ar.reshape(S5_SLABS, S5_SLAB_ST), ai.reshape(S5_SLABS, S5_SLAB_ST)], axis=0)
    return wb, wc, avec


def _s5_call(x, win, wb, wc, avec, batch, reverse, tail):
    rows = x.shape[0]
    n = rows // MIX_ROWS
    steps = MIX_ROWS // batch
    final = tail is not None
    if reverse:
        tok = pl.BlockSpec((MIX_ROWS, D_MODEL), lambda i: (n - 1 - i, 0))
    else:
        tok = pl.BlockSpec((MIX_ROWS, D_MODEL), lambda i: (i, 0))
    args = [x, win, wb, wc, avec]
    specs = [tok, _const_spec(win.shape), _const_spec(wb.shape), _const_spec(wc.shape), _const_spec(avec.shape)]
    if final:
        yprev, dskip, wglu, lng, lnb = tail
        args += [yprev, dskip, wglu, lng, lnb]
        specs += [tok, _const_spec(dskip.shape), _const_spec(wglu.shape), _const_spec(lng.shape),
                  _const_spec(lnb.shape)]
    return pl.pallas_call(
        functools.partial(_s5_kernel, batch=batch, steps=steps, reverse=reverse, final=final),
        grid=(n,),
        in_specs=specs,
        out_specs=tok,
        out_shape=jax.ShapeDtypeStruct((rows, D_MODEL), F32),
        scratch_shapes=[pltpu.VMEM((MIX_ROWS, S5_SLABS * 2 * S5_SLAB_ST), F32),
                        pltpu.VMEM((batch, S5_SLABS * 2 * S5_SLAB_ST), F32)],
        compiler_params=_cparams("arbitrary"),
        name="s5_bwd_glu_ln" if final else "s5_fwd",
    )(*args)


def _s5_layer(x, batch, w_in, lam_re, lam_im, log_step, b_re, b_im, c_re, c_im, d_skip, w_glu, ln_g, ln_b):
    win = w_in.astype(BF16)
    wb0, wc0, av0 = _s5_weights(lam_re[0], lam_im[0], log_step[0], b_re[0], b_im[0], c_re[0], c_im[0])
    wb1, wc1, av1 = _s5_weights(lam_re[1], lam_im[1], log_step[1], b_re[1], b_im[1], c_re[1], c_im[1])
    y_fwd = _s5_call(x, win, wb0, wc0, av0, batch, False, None)
    tail = (y_fwd, d_skip.astype(F32)[None, :], w_glu.astype(BF16), ln_g[None, :], ln_b[None, :])
    return _s5_call(x, win, wb1, wc1, av1, batch, True, tail)


def _rg_kernel(*refs, batch, steps, reverse, final, n_chunks):
    if final:
        (x_ref, xp_ref, xn_ref, wr_ref, cw_ref, cb_ref, wa_ref, ba_ref, wx_ref, bx_ref, sp_ref,
         hprev_ref, wg_ref, wo_ref, lng_ref, lnb_ref, out_ref, xe_scr, a_scr, b_scr, h_scr) = refs
    else:
        (x_ref, xp_ref, xn_ref, wr_ref, cw_ref, cb_ref, wa_ref, ba_ref, wx_ref, bx_ref, sp_ref,
         out_ref, xe_scr, a_scr, b_scr, h_scr) = refs
    rows = steps * batch
    i = pl.program_id(0)
    chunk = (n_chunks - 1 - i) if reverse else i

    @pl.when(i == 0)
    def _():
        h_scr[...] = jnp.zeros_like(h_scr)

    x = x_ref[...]
    keep_prev = (chunk > 0).astype(F32)
    keep_next = (chunk < n_chunks - 1).astype(F32)
    xe_scr[0:batch, :] = xp_ref[...] * keep_prev
    xe_scr[batch:batch + rows, :] = x
    xe_scr[batch + rows:, :] = xn_ref[...] * keep_next
    r_ext = _mm(xe_scr[...], wr_ref[...])
    xe_scr[...] = r_ext

    c = cb_ref[...] + xe_scr[0:rows, :] * cw_ref[0:1, :]
    for k in range(1, RG_CONV):
        c = c + xe_scr[k * batch:k * batch + rows, :] * cw_ref[k:k + 1, :]

    cb16 = c.astype(BF16)
    ga = []
    gx = []
    for blk in range(RG_BLOCKS):
        cs = cb16[:, blk * RG_BLOCK:(blk + 1) * RG_BLOCK]
        ga.append(jnp.dot(cs, wa_ref[blk], preferred_element_type=F32))
        gx.append(jnp.dot(cs, wx_ref[blk], preferred_element_type=F32))
    r_gate = jax.nn.sigmoid(jnp.concatenate(ga, axis=1) + ba_ref[...])
    i_gate = jax.nn.sigmoid(jnp.concatenate(gx, axis=1) + bx_ref[...])
    log_a = -RG_C * r_gate * sp_ref[...]
    a_scr[...] = jnp.exp(log_a)
    th = jnp.tanh(log_a)
    b_scr[...] = jnp.sqrt(-2.0 * th / (1.0 - th)) * (i_gate * c)

    def step(k, h):
        t = (steps - 1 - k) if reverse else k
        row = pl.multiple_of(t * batch, batch)
        h = a_scr[pl.ds(row, batch), :] * h + b_scr[pl.ds(row, batch), :]
        b_scr[pl.ds(row, batch), :] = h
        return h

    h_scr[...] = lax.fori_loop(0, steps, step, h_scr[...], unroll=4)

    if not final:
        out_ref[...] = b_scr[...]
        return

    h_tot = b_scr[...] + hprev_ref[...]
    g = _mm(x, wg_ref[...])
    y = h_tot * _gelu(g)
    mix = _mm(y, wo_ref[...])
    out_ref[...] = _layernorm(ALPHA * x + mix, lng_ref[...], lnb_ref[...])


def _rg_call(x, wr, cw, cb, wa, ba, wx, bx, sp, batch, reverse, tail):
    rows = x.shape[0]
    chunk = math.gcd(rows, RG_ROWS)
    n = rows // chunk
    steps = chunk // batch
    final = tail is not None
    per_prev = chunk // batch
    per_next = chunk // (2 * batch)
    last_next = rows // (2 * batch) - 1

    def cidx(i):
        return (n - 1 - i) if reverse else i

    tok = pl.BlockSpec((chunk, D_MODEL), lambda i: (cidx(i), 0))
    prev = pl.BlockSpec((batch, D_MODEL), lambda i: (jnp.maximum(cidx(i) * per_prev - 1, 0), 0))
    nxt = pl.BlockSpec((2 * batch, D_MODEL), lambda i: (jnp.minimum((cidx(i) + 1) * per_next, last_next), 0))
    args = [x, x, x, wr, cw, cb, wa, ba, wx, bx, sp]
    specs = [tok, prev, nxt] + [_const_spec(a.shape) for a in args[3:]]
    if final:
        args += list(tail)
        specs += [tok] + [_const_spec(a.shape) for a in tail[1:]]
    return pl.pallas_call(
        functools.partial(_rg_kernel, batch=batch, steps=steps, reverse=reverse, final=final, n_chunks=n),
        grid=(n,),
        in_specs=specs,
        out_specs=tok,
        out_shape=jax.ShapeDtypeStruct((rows, D_MODEL), F32),
        scratch_shapes=[pltpu.VMEM((chunk + 3 * batch, D_MODEL), F32),
                        pltpu.VMEM((chunk, D_MODEL), F32),
                        pltpu.VMEM((chunk, D_MODEL), F32),
                        pltpu.VMEM((batch, D_MODEL), F32)],
        compiler_params=_cparams("arbitrary"),
        name="rg_bwd_out_ln" if final else "rg_fwd",
    )(*args)


def _rg_layer(x, batch, w_in, conv_w, conv_b, w_ga, b_ga, w_gx, b_gx, lam, w_out, ln_g, ln_b):
    wg = w_in[:, :D_MODEL].astype(BF16)
    wr = w_in[:, D_MODEL:].astype(BF16)
    cw = jnp.concatenate([conv_w.astype(F32), jnp.zeros((8 - RG_CONV, D_MODEL), F32)], axis=0)
    cb = conv_b.astype(F32)[None, :]
    sp = jax.nn.softplus(-lam.astype(F32))
    common = lambda d: (wr, cw, cb, w_ga[d].astype(BF16), b_ga[d][None, :].astype(F32),
                        w_gx[d].astype(BF16), b_gx[d][None, :].astype(F32), sp[d][None, :])
    h_fwd = _rg_call(x, *common(0), batch, False, None)
    tail = (h_fwd, wg, w_out.astype(BF16), ln_g[None, :], ln_b[None, :])
    return _rg_call(x, *common(1), batch, True, tail)


def _sorting_pairs(n):
    pairs = []
    p = 1
    while p < n:
        k = p
        while k >= 1:
            for j in range(k % p, n - k, 2 * k):
                for i in range(min(k, n - j - k)):
                    if (i + j) // (2 * p) == (i + j + k) // (2 * p):
                        pairs.append((i + j, i + j + k))
            k //= 2
        p *= 2
    return pairs


_SORT16 = _sorting_pairs(16)
_SORT8 = _sorting_pairs(8)


def _exchange(x, i, j):
    x[i], x[j] = jnp.maximum(x[i], x[j]), jnp.minimum(x[i], x[j])


def _bitonic_merge(x):
    d = len(x) // 2
    while d >= 1:
        for i in range(len(x)):
            if i & d == 0:
                _exchange(x, i, i + d)
        d //= 2


def _merge_top16_across_sublanes(x, shifts):
    for shift in shifts:
        y = [pltpu.roll(v, shift, 0) for v in x]
        x = [jnp.maximum(x[k], y[PEER_TOPK - 1 - k]) for k in range(PEER_TOPK)]
        _bitonic_merge(x)
    return x


def _count_ge(tiles, thr):
    cnt = None
    for t in tiles:
        c = jnp.where(t >= thr, 1.0, 0.0)
        cnt = c if cnt is None else cnt + c
    for shift in (4, 2, 1):
        cnt = cnt + pltpu.roll(cnt, shift, 0)
    return cnt


def _peer_topk_fast(sk_ref, an_ref, br_ref, qt_scr, tq):
    neg_inf = jnp.float32(-jnp.inf)
    sub_iota = lax.broadcasted_iota(jnp.int32, (8, LANES), 0)
    sub_is = [sub_iota == r for r in range(8)]

    def head_body(h, bad):
        for lt in range(tq // LANES):
            lanes = slice(lt * LANES, (lt + 1) * LANES)
            tiles = []
            tops = []
            for c in range(2):
                row0 = pl.multiple_of((2 * h + c) * PEER_HALF, PEER_HALF)
                q = qt_scr[pl.ds(row0, PEER_HALF), lanes]
                s = jnp.dot(sk_ref[c], q.astype(BF16), preferred_element_type=F32)
                t = [s[v * 8:(v + 1) * 8, :] for v in range(PEER_NKEYS // 8)]
                x = list(t)
                for i, j in _SORT16:
                    _exchange(x, i, j)
                tiles.append(t)
                tops.append(_merge_top16_across_sublanes(x, (4, 2, 1)))
            v0, v1 = tops

            cand = [[v0[a] + v1[b] for b in range(_PAIR_COUNTS[a])] for a in range(PEER_TOPK)]
            flat = [cand[a][b] for a in range(PEER_TOPK) for b in range(_PAIR_COUNTS[a])]
            packed = []
            for v in range(_PAIR_ROWS // 8):
                p = jnp.full((8, LANES), neg_inf, F32)
                for r in range(8):
                    if v * 8 + r < _N_PAIRS:
                        p = jnp.where(sub_is[r], flat[v * 8 + r], p)
                packed.append(p)
            x = packed + [jnp.full((8, LANES), neg_inf, F32)]
            for i, j in _SORT8:
                _exchange(x, i, j)
            y = [pltpu.roll(v, 4, 0) for v in x]
            x = x + y[::-1]
            _bitonic_merge(x)
            x = _merge_top16_across_sublanes(x, (2, 1))
            tau = x[PEER_TOPK - 1]
            bad = bad + jnp.where(_count_ge(packed, tau) != float(PEER_TOPK), 1.0, 0.0)

            top = flat[0]
            z = None
            for p in packed:
                e = jnp.where(p >= tau, jnp.exp(p - top), 0.0)
                z = e if z is None else z + e
            for shift in (4, 2, 1):
                z = z + pltpu.roll(z, shift, 0)
            inv_z = 1.0 / z

            counts = []
            for a in range(PEER_TOPK):
                n_a = None
                for b in range(_PAIR_COUNTS[a]):
                    c = jnp.where(cand[a][b] >= tau, 1.0, 0.0)
                    n_a = c if n_a is None else n_a + c
                counts.append(n_a)

            over0 = _count_ge(tiles[0], v0[PEER_TOPK - 1]) > float(PEER_TOPK)
            over1 = _count_ge(tiles[1], v1[PEER_TOPK - 1]) > float(PEER_TOPK)
            bad = bad + jnp.where(over0, counts[PEER_TOPK - 1], 0.0)
            for a in range(PEER_TOPK - 1):
                bad = bad + jnp.where(v0[a] == v0[a + 1], jnp.abs(counts[a] - counts[a + 1]), 0.0)
            for a in range(PEER_TOPK):
                edge = _PAIR_COUNTS[a]
                full = counts[a] == float(edge)
                if edge < PEER_TOPK:
                    tie = v1[edge - 1] == v1[edge]
                else:
                    tie = over1
                bad = bad + jnp.where(tie, jnp.where(full, 1.0, 0.0), 0.0)

            for v in range(PEER_NKEYS // 8):
                rows_ = slice(v * 8, (v + 1) * 8)
                s0 = tiles[0][v]
                s1 = tiles[1][v]
                n_dense = jnp.zeros((8, LANES), F32)
                rank1 = jnp.zeros((8, LANES), F32)
                for a in range(PEER_TOPK):
                    n_dense = jnp.where(s0 == v0[a], counts[a], n_dense)
                    rank1 = rank1 + jnp.where(v1[a] > s1, 1.0, 0.0)
                an_ref[0, h, rows_, lanes] = jnp.exp(s0 - v0[0]) * inv_z
                an_ref[1, h, rows_, lanes] = n_dense
                br_ref[0, h, rows_, lanes] = jnp.exp(s1 - v1[0]).astype(BF16)
                br_ref[1, h, rows_, lanes] = rank1.astype(BF16)
        return bad

    return lax.fori_loop(0, PEER_HEADS, head_body, jnp.zeros((8, LANES), F32))


def _peer_topk_exact(sk_ref, an_ref, br_ref, qt_scr, vals_scr, rank_scr, e_scr, cand_scr, sel_scr, tq):
    neg_inf = jnp.float32(-jnp.inf)
    n_lt = tq // LANES
    iota_k = lax.broadcasted_iota(jnp.int32, (PEER_NKEYS, LANES), 0).astype(F32)

    def half_body(hc, _):
        q = qt_scr[pl.ds(pl.multiple_of(hc * PEER_HALF, PEER_HALF), PEER_HALF), :]
        s = jnp.dot(sk_ref[hc % 2], q.astype(BF16), preferred_element_type=F32)
        for lt in range(n_lt):
            lanes = slice(lt * LANES, (lt + 1) * LANES)
            s_l = s[:, lanes]

            work = s_l
            rank = jnp.full((PEER_NKEYS, LANES), float(PEER_NKEYS), F32)
            for r in range(PEER_TOPK):
                m = jnp.max(work, axis=0, keepdims=True)
                pos = jnp.min(jnp.where(work == m, iota_k, float(PEER_NKEYS)), axis=0, keepdims=True)
                hit = iota_k == pos
                vals_scr[hc, r:r + 1, lanes] = m
                work = jnp.where(hit, neg_inf, work)
                rank = jnp.where(hit, float(r), rank)
            top = vals_scr[hc, 0:1, lanes]
            rank_scr[hc, :, lanes] = rank
            e_scr[hc, :, lanes] = jnp.where(rank < PEER_TOPK, jnp.exp(s_l - top), 0.0)
        return 0

    lax.fori_loop(0, 2 * PEER_HEADS, half_body, 0)

    iota_p = lax.broadcasted_iota(jnp.int32, (_PAIR_ROWS, LANES), 0).astype(F32)

    def head_body(h, _):
        for lt in range(n_lt):
            lanes = slice(lt * LANES, (lt + 1) * LANES)
            v0 = vals_scr[2 * h, :, lanes]
            v1 = vals_scr[2 * h + 1, :, lanes]
            cand_scr[...] = jnp.full((_PAIR_ROWS, LANES), neg_inf, F32)
            for a in range(PEER_TOPK):
                cand_scr[_PAIR_OFFS[a]:_PAIR_OFFS[a] + _PAIR_COUNTS[a], :] = (
                    v0[a:a + 1, :] + v1[0:_PAIR_COUNTS[a], :])
            cand = cand_scr[...]

            def extract(r, carry):
                work, sel = carry
                m = jnp.max(work, axis=0, keepdims=True)
                pos = jnp.min(jnp.where(work == m, iota_p, float(_PAIR_ROWS)), axis=0, keepdims=True)
                hit = iota_p == pos
                return jnp.where(hit, neg_inf, work), jnp.where(hit, 1.0, sel)

            _, sel = lax.fori_loop(0, PEER_TOPK, extract, (cand, jnp.zeros((_PAIR_ROWS, LANES), F32)))
            sel_scr[...] = sel
            top = cand_scr[0:1, :]
            z = jnp.sum(jnp.where(sel > 0.0, jnp.exp(cand - top), 0.0), axis=0, keepdims=True)
            inv_z = 1.0 / z
            rank0 = rank_scr[2 * h, :, lanes]
            n_dense = jnp.zeros((PEER_NKEYS, LANES), F32)
            for a in range(PEER_TOPK):
                n_a = jnp.sum(sel_scr[_PAIR_OFFS[a]:_PAIR_OFFS[a] + _PAIR_COUNTS[a], :], axis=0, keepdims=True)
                n_dense = jnp.where(rank0 == float(a), n_a, n_dense)
            an_ref[0, h, :, lanes] = e_scr[2 * h, :, lanes] * inv_z
            an_ref[1, h, :, lanes] = n_dense
            br_ref[0, h, :, lanes] = e_scr[2 * h + 1, :, lanes].astype(BF16)
            br_ref[1, h, :, lanes] = rank_scr[2 * h + 1, :, lanes].astype(BF16)
        return 0

    lax.fori_loop(0, PEER_HEADS, head_body, 0)


def _peer_topk_kernel(x_ref, wqt_ref, sk_ref, an_ref, br_ref, qt_scr, vals_scr, rank_scr, e_scr, cand_scr,
                      sel_scr, *, tq):
    qt_scr[...] = lax.dot_general(wqt_ref[...], x_ref[...].astype(BF16), (((1,), (1,)), ((), ())),
                                  preferred_element_type=F32)
    bad = _peer_topk_fast(sk_ref, an_ref, br_ref, qt_scr, tq)

    @pl.when(jnp.max(bad) > 0.0)
    def _():
        _peer_topk_exact(sk_ref, an_ref, br_ref, qt_scr, vals_scr, rank_scr, e_scr, cand_scr, sel_scr, tq)


def _peer_topk(x, wqt, sk, tq):
    rows = x.shape[0]
    return pl.pallas_call(
        functools.partial(_peer_topk_kernel, tq=tq),
        grid=(rows // tq,),
        in_specs=[pl.BlockSpec((tq, D_MODEL), lambda i: (i, 0)), _const_spec(wqt.shape), _const_spec(sk.shape)],
        out_specs=[pl.BlockSpec((2, PEER_HEADS, PEER_NKEYS, tq), lambda i: (0, 0, 0, i)),
                   pl.BlockSpec((2, PEER_HEADS, PEER_NKEYS, tq), lambda i: (0, 0, 0, i))],
        out_shape=[jax.ShapeDtypeStruct((2, PEER_HEADS, PEER_NKEYS, rows), F32),
                   jax.ShapeDtypeStruct((2, PEER_HEADS, PEER_NKEYS, rows), BF16)],
        scratch_shapes=[pltpu.VMEM((2 * PEER_HEADS * PEER_HALF, tq), F32),
                        pltpu.VMEM((2 * PEER_HEADS, PEER_TOPK, tq), F32),
                        pltpu.VMEM((2 * PEER_HEADS, PEER_NKEYS, tq), F32),
                        pltpu.VMEM((2 * PEER_HEADS, PEER_NKEYS, tq), F32),
                        pltpu.VMEM((_PAIR_ROWS, LANES), F32),
                        pltpu.VMEM((_PAIR_ROWS, LANES), F32)],
        compiler_params=_cparams("arbitrary"),
        name="peer_topk",
    )(x, wqt, sk)


PEER_CHUNK = 8 * PEER_NKEYS


PEER_STEP_CHUNKS = 2


def _peer_dense_kernel(x_ref, an_ref, br_ref, u_ref, vt_ref, p_ref, lng_ref, lnb_ref, wp_ref, wg_ref, o_ref,
                       xt_scr, acc_scr, br_scr, act_scr, wt_scr, *, tq):
    e = pl.program_id(1)

    @pl.when(e == 0)
    def _():
        xt_scr[...] = x_ref[...].T.astype(BF16)
        acc_scr[...] = jnp.zeros_like(acc_scr)
        br_scr[...] = br_ref[...]

    def up(q):
        rows_ = slice(2 * q * PEER_NKEYS, 2 * (q + 1) * PEER_NKEYS)
        act_scr[rows_, :] = jnp.dot(u_ref[rows_, :], xt_scr[...], preferred_element_type=F32)

    def gate_key(ii):
        rows_ = slice(ii * PEER_NKEYS, (ii + 1) * PEER_NKEYS)
        for lt in range(tq // LANES):
            lanes = slice(lt * LANES, (lt + 1) * LANES)
            gate = None
            for h in range(PEER_HEADS):
                a = jnp.broadcast_to(an_ref[0, h, ii:ii + 1, lanes], (PEER_NKEYS, LANES)).astype(BF16)
                n = jnp.broadcast_to(an_ref[1, h, ii:ii + 1, lanes], (PEER_NKEYS, LANES)).astype(BF16)
                term = a * jnp.where(br_scr[1, h, :, lanes] < n, br_scr[0, h, :, lanes], 0)
                gate = term if gate is None else gate + term
            wt_scr[rows_, lanes] = gate * _gelu(act_scr[rows_, lanes]).astype(BF16)

    def down(c):
        cols = slice(c * PEER_CHUNK, (c + 1) * PEER_CHUNK)
        acc_scr[...] += jnp.dot(vt_ref[:, cols], wt_scr[cols, :], preferred_element_type=F32)

    n_up = 4 * PEER_STEP_CHUNKS
    up(0)
    for q in range(n_up):
        if q + 1 < n_up:
            up(q + 1)
        gate_key(2 * q)
        gate_key(2 * q + 1)
        if q % 4 == 3:
            down(q // 4)

    @pl.when(e == pl.num_programs(1) - 1)
    def _():
        x2 = _layernorm(ALPHA * x_ref[...] + acc_scr[...].T, lng_ref[...], lnb_ref[...])
        o_ref[...] = x2 + _mm(p_ref[...], wp_ref[...]) * jax.nn.sigmoid(_mm(x2, wg_ref[...]))


def _peer_dense(x, an, br, u16, vt16, p, lng, lnb, wp, wg, tq):
    rows = x.shape[0]
    step = PEER_STEP_CHUNKS * PEER_CHUNK
    return pl.pallas_call(
        functools.partial(_peer_dense_kernel, tq=tq),
        grid=(rows // tq, PEER_EXPERTS // step),
        in_specs=[pl.BlockSpec((tq, D_MODEL), lambda t, e: (t, 0)),
                  pl.BlockSpec((2, PEER_HEADS, 8 * PEER_STEP_CHUNKS, tq), lambda t, e: (0, 0, e, t)),
                  pl.BlockSpec((2, PEER_HEADS, PEER_NKEYS, tq), lambda t, e: (0, 0, 0, t)),
                  pl.BlockSpec((step, D_MODEL), lambda t, e: (e, 0)),
                  pl.BlockSpec((D_MODEL, step), lambda t, e: (0, e)),
                  pl.BlockSpec((tq, PLE_DIM), lambda t, e: (t, 0)),
                  _const_spec(lng.shape), _const_spec(lnb.shape), _const_spec(wp.shape), _const_spec(wg.shape)],
        out_specs=pl.BlockSpec((tq, D_MODEL), lambda t, e: (t, 0)),
        out_shape=jax.ShapeDtypeStruct((rows, D_MODEL), F32),
        scratch_shapes=[pltpu.VMEM((D_MODEL, tq), BF16),
                        pltpu.VMEM((D_MODEL, tq), F32),
                        pltpu.VMEM((2, PEER_HEADS, PEER_NKEYS, tq), BF16),
                        pltpu.VMEM((step, tq), F32),
                        pltpu.VMEM((step, tq), BF16)],
        compiler_params=_cparams("arbitrary", "arbitrary"),
        name="peer_dense_ln_ple",
    )(x, an, br, u16, vt16, p, lng, lnb, wp, wg)


def _token_tile(rows, want):
    return math.gcd(rows, want)


def _trunk(x, p, s5, rg, ln, peer, ple):
    batch, seq, _ = x.shape
    rows = batch * seq
    ln1_g, ln1_b, ln2_g, ln2_b = ln
    peer_w_q, peer_subkeys, peer_u, peer_v = peer
    ple_w_proj, ple_w_gate = ple
    xt = jnp.transpose(x, (1, 0, 2)).reshape(rows, D_MODEL)
    pt = jnp.transpose(p, (0, 2, 1, 3)).reshape(DEPTH, rows, PLE_DIM)
    for i in range(DEPTH):
        j = i // 2
        if i % 2 == 0:
            xt = _s5_layer(xt, batch, *(w[j] for w in s5), ln1_g[i], ln1_b[i])
        else:
            xt = _rg_layer(xt, batch, *(w[j] for w in rg), ln1_g[i], ln1_b[i])
        an, br = _peer_topk(xt, peer_w_q[i].T.astype(BF16), peer_subkeys[i].astype(BF16), _token_tile(rows, 512))
        xt = _peer_dense(xt, an, br, peer_u[i].astype(BF16), peer_v[i].T.astype(BF16), pt[i],
                         ln2_g[i][None, :], ln2_b[i][None, :], ple_w_proj[i].astype(BF16),
                         ple_w_gate[i].astype(BF16), _token_tile(rows, 512))
    return jnp.transpose(xt.reshape(seq, batch, D_MODEL), (1, 0, 2))


def kernel(x_prompt, x_sample, p_prompt, p_sample, s5_w_in, s5_lam_re, s5_lam_im, s5_log_step, s5_b_re, s5_b_im, s5_c_re, s5_c_im, s5_d, s5_w_glu, rg_w_in, rg_conv_w, rg_conv_b, rg_w_gate_a, rg_b_gate_a, rg_w_gate_x, rg_b_gate_x, rg_lambda, rg_w_out, ln1_g, ln1_b, ln2_g, ln2_b, peer_w_q, peer_subkeys, peer_u, peer_v, ple_w_proj, ple_w_gate):
    s5 = (s5_w_in, s5_lam_re, s5_lam_im, s5_log_step, s5_b_re, s5_b_im, s5_c_re, s5_c_im, s5_d, s5_w_glu)
    rg = (rg_w_in, rg_conv_w, rg_conv_b, rg_w_gate_a, rg_b_gate_a, rg_w_gate_x, rg_b_gate_x, rg_lambda, rg_w_out)
    ln = (ln1_g, ln1_b, ln2_g, ln2_b)
    peer = (peer_w_q, peer_subkeys, peer_u, peer_v)
    ple = (ple_w_proj, ple_w_gate)
    y_prompt = _trunk(x_prompt, p_prompt, s5, rg, ln, peer, ple)
    y_sample = _trunk(x_sample, p_sample, s5, rg, ln, peer, ple)
    return (y_prompt, y_sample)
```

```python
import functools
import math

import jax
import jax.numpy as jnp
from jax import lax
from jax.experimental import pallas as pl
from jax.experimental.pallas import tpu as pltpu

F32 = jnp.float32
BF16 = jnp.bfloat16

D_MODEL = 1024
DEPTH = 2
S5_GROUP = 16
S5_GROUPS = 64
S5_STATE = 64
S5_SLABS = 4
S5_SLAB_CH = D_MODEL // S5_SLABS
S5_SLAB_ST = 16 * S5_STATE
RG_BLOCKS = 4
RG_BLOCK = 256
RG_CONV = 4
RG_C = 8.0
PEER_HEADS = 8
PEER_NKEYS = 128
PEER_EXPERTS = PEER_NKEYS * PEER_NKEYS
PEER_HALF = 128
PEER_TOPK = 16
PLE_DIM = 256
ALPHA = (2 * DEPTH) ** 0.25
LN_EPS = 1e-5

LANES = 128
VMEM_LIMIT = 56 * 1024 * 1024
MIX_ROWS = 512
RG_ROWS = 1024
SCAN_COLS = 512

_PAIR_COUNTS = [min(PEER_TOPK, PEER_TOPK // (a + 1)) for a in range(PEER_TOPK)]
_PAIR_OFFS = [sum(_PAIR_COUNTS[:a]) for a in range(PEER_TOPK)]
_N_PAIRS = sum(_PAIR_COUNTS)
_PAIR_ROWS = ((_N_PAIRS + 7) // 8) * 8


def _mm(a, b):
    return jnp.dot(a.astype(BF16), b.astype(BF16), preferred_element_type=F32)


def _gelu(x):
    k = math.sqrt(2.0 / math.pi)
    hx = 0.5 * x
    return hx + hx * jnp.tanh(x * (k + (k * 0.044715) * (x * x)))


def _layernorm(z, g, b):
    mu = jnp.mean(z, axis=-1, keepdims=True)
    zc = z - mu
    var = jnp.mean(zc * zc, axis=-1, keepdims=True)
    return zc * lax.rsqrt(var + LN_EPS) * g + b


def _cparams(*sem):
    return pltpu.CompilerParams(dimension_semantics=sem, vmem_limit_bytes=VMEM_LIMIT)


def _const_spec(shape):
    nd = len(shape)
    return pl.BlockSpec(shape, lambda *_: (0,) * nd)


def _s5_kernel(*refs, batch, steps, reverse, final):
    if final:
        (x_ref, win_ref, wb_ref, wc_ref, a_ref, yprev_ref, d_ref, wglu_ref, lng_ref, lnb_ref,
         out_ref, bu_scr, h_scr) = refs
    else:
        x_ref, win_ref, wb_ref, wc_ref, a_ref, out_ref, bu_scr, h_scr = refs

    @pl.when(pl.program_id(0) == 0)
    def _():
        h_scr[...] = jnp.zeros_like(h_scr)

    x = x_ref[...]
    u = _mm(x, win_ref[...])
    ub = u.astype(BF16)
    for s in range(S5_SLABS):
        bu_scr[:, s * 2 * S5_SLAB_ST:(s + 1) * 2 * S5_SLAB_ST] = jnp.dot(
            ub[:, s * S5_SLAB_CH:(s + 1) * S5_SLAB_CH], wb_ref[s], preferred_element_type=F32)

    for s in range(S5_SLABS):
        for cb in range(S5_SLAB_ST // SCAN_COLS):
            c_re = s * 2 * S5_SLAB_ST + cb * SCAN_COLS
            c_im = c_re + S5_SLAB_ST
            a_re = jnp.broadcast_to(a_ref[s:s + 1, cb * SCAN_COLS:(cb + 1) * SCAN_COLS], (batch, SCAN_COLS))
            a_im = jnp.broadcast_to(
                a_ref[S5_SLABS + s:S5_SLABS + s + 1, cb * SCAN_COLS:(cb + 1) * SCAN_COLS], (batch, SCAN_COLS))

            def step(k, carry, c_re=c_re, c_im=c_im, a_re=a_re, a_im=a_im):
                h_re, h_im = carry
                t = (steps - 1 - k) if reverse else k
                row = pl.multiple_of(t * batch, batch)
                b_re = bu_scr[pl.ds(row, batch), c_re:c_re + SCAN_COLS]
                b_im = bu_scr[pl.ds(row, batch), c_im:c_im + SCAN_COLS]
                n_re = a_re * h_re - a_im * h_im + b_re
                n_im = a_re * h_im + a_im * h_re + b_im
                bu_scr[pl.ds(row, batch), c_re:c_re + SCAN_COLS] = n_re
                bu_scr[pl.ds(row, batch), c_im:c_im + SCAN_COLS] = n_im
                return n_re, n_im

            h_re, h_im = lax.fori_loop(
                0, steps, step,
                (h_scr[:, c_re:c_re + SCAN_COLS], h_scr[:, c_im:c_im + SCAN_COLS]), unroll=4)
            h_scr[:, c_re:c_re + SCAN_COLS] = h_re
            h_scr[:, c_im:c_im + SCAN_COLS] = h_im

    ys = []
    for s in range(S5_SLABS):
        hb = bu_scr[:, s * 2 * S5_SLAB_ST:(s + 1) * 2 * S5_SLAB_ST].astype(BF16)
        ys.append(jnp.dot(hb, wc_ref[s], preferred_element_type=F32))
    y = jnp.concatenate(ys, axis=1)

    if not final:
        out_ref[...] = y
        return

    y = y + yprev_ref[...] + d_ref[...] * u
    hg = _gelu(y)
    vg = _mm(hg, wglu_ref[...])
    mix = vg[:, :D_MODEL] * jax.nn.sigmoid(vg[:, D_MODEL:])
    out_ref[...] = _layernorm(ALPHA * x + mix, lng_ref[...], lnb_ref[...])


def _s5_discretize(lam_re, lam_im, log_step, b_re, b_im):
    step = jnp.exp(log_step.astype(F32))[:, None]
    lr = lam_re.astype(F32)
    li = lam_im.astype(F32)
    mag = jnp.exp(lr * step)
    ang = li * step
    ar = mag * jnp.cos(ang)
    ai = mag * jnp.sin(ang)
    den = lr * lr + li * li
    zr = ar - 1.0
    qr = (zr * lr + ai * li) / den
    qi = (ai * lr - zr * li) / den
    br = b_re.astype(F32)
    bi = b_im.astype(F32)
    bbr = qr[..., None] * br - qi[..., None] * bi
    bbi = qr[..., None] * bi + qi[..., None] * br
    return ar, ai, bbr, bbi


def _s5_weights(lam_re, lam_im, log_step, b_re, b_im, c_re, c_im):
    ar, ai, bbr, bbi = _s5_discretize(lam_re, lam_im, log_step, b_re, b_im)
    eye = jnp.eye(16, dtype=F32)

    def in_blocks(bb):
        bb4 = bb.reshape(S5_SLABS, 16, S5_STATE, S5_GROUP)
        return jnp.einsum('sgpc,gh->sgchp', bb4, eye).reshape(S5_SLABS, S5_SLAB_CH, S5_SLAB_ST)

    def out_blocks(cc):
        cc4 = cc.astype(F32).reshape(S5_SLABS, 16, S5_GROUP, S5_STATE)
        return jnp.einsum('sgcp,gh->sgphc', cc4, eye).reshape(S5_SLABS, S5_SLAB_ST, S5_SLAB_CH)

    wb = jnp.concatenate([in_blocks(bbr), in_blocks(bbi)], axis=2).astype(BF16)
    wc = jnp.concatenate([out_blocks(c_re), -out_blocks(c_im)], axis=1).astype(BF16)
    avec = jnp.concatenate([ar.reshape(S5_SLABS, S5_SLAB_ST), ai.reshape(S5_SLABS, S5_SLAB_ST)], axis=0)
    return wb, wc, avec


def _s5_call(x, win, wb, wc, avec, batch, reverse, tail):
    rows = x.shape[0]
    n = rows // MIX_ROWS
    steps = MIX_ROWS // batch
    final = tail is not None
    if reverse:
        tok = pl.BlockSpec((MIX_ROWS, D_MODEL), lambda i: (n - 1 - i, 0))
    else:
        tok = pl.BlockSpec((MIX_ROWS, D_MODEL), lambda i: (i, 0))
    args = [x, win, wb, wc, avec]
    specs = [tok, _const_spec(win.shape), _const_spec(wb.shape), _const_spec(wc.shape), _const_spec(avec.shape)]
    if final:
        yprev, dskip, wglu, lng, lnb = tail
        args += [yprev, dskip, wglu, lng, lnb]
        specs += [tok, _const_spec(dskip.shape), _const_spec(wglu.shape), _const_spec(lng.shape),
                  _const_spec(lnb.shape)]
    return pl.pallas_call(
        functools.partial(_s5_kernel, batch=batch, steps=steps, reverse=reverse, final=final),
        grid=(n,),
        in_specs=specs,
        out_specs=tok,
        out_shape=jax.ShapeDtypeStruct((rows, D_MODEL), F32),
        scratch_shapes=[pltpu.VMEM((MIX_ROWS, S5_SLABS * 2 * S5_SLAB_ST), F32),
                        pltpu.VMEM((batch, S5_SLABS * 2 * S5_SLAB_ST), F32)],
        compiler_params=_cparams("arbitrary"),
        name="s5_bwd_glu_ln" if final else "s5_fwd",
    )(*args)


def _s5_layer(x, batch, w_in, lam_re, lam_im, log_step, b_re, b_im, c_re, c_im, d_skip, w_glu, ln_g, ln_b):
    win = w_in.astype(BF16)
    wb0, wc0, av0 = _s5_weights(lam_re[0], lam_im[0], log_step[0], b_re[0], b_im[0], c_re[0], c_im[0])
    wb1, wc1, av1 = _s5_weights(lam_re[1], lam_im[1], log_step[1], b_re[1], b_im[1], c_re[1], c_im[1])
    y_fwd = _s5_call(x, win, wb0, wc0, av0, batch, False, None)
    tail = (y_fwd, d_skip.astype(F32)[None, :], w_glu.astype(BF16), ln_g[None, :], ln_b[None, :])
    return _s5_call(x, win, wb1, wc1, av1, batch, True, tail)


def _rg_kernel(*refs, batch, steps, reverse, final, n_chunks):
    if final:
        (x_ref, xp_ref, xn_ref, wr_ref, cw_ref, cb_ref, wa_ref, ba_ref, wx_ref, bx_ref, sp_ref,
         hprev_ref, wg_ref, wo_ref, lng_ref, lnb_ref, out_ref, xe_scr, a_scr, b_scr, h_scr) = refs
    else:
        (x_ref, xp_ref, xn_ref, wr_ref, cw_ref, cb_ref, wa_ref, ba_ref, wx_ref, bx_ref, sp_ref,
         out_ref, xe_scr, a_scr, b_scr, h_scr) = refs
    rows = steps * batch
    i = pl.program_id(0)
    chunk = (n_chunks - 1 - i) if reverse else i

    @pl.when(i == 0)
    def _():
        h_scr[...] = jnp.zeros_like(h_scr)

    x = x_ref[...]
    keep_prev = (chunk > 0).astype(F32)
    keep_next = (chunk < n_chunks - 1).astype(F32)
    xe_scr[0:batch, :] = xp_ref[...] * keep_prev
    xe_scr[batch:batch + rows, :] = x
    xe_scr[batch + rows:, :] = xn_ref[...] * keep_next
    r_ext = _mm(xe_scr[...], wr_ref[...])
    xe_scr[...] = r_ext

    c = cb_ref[...] + xe_scr[0:rows, :] * cw_ref[0:1, :]
    for k in range(1, RG_CONV):
        c = c + xe_scr[k * batch:k * batch + rows, :] * cw_ref[k:k + 1, :]

    cb16 = c.astype(BF16)
    ga = []
    gx = []
    for blk in range(RG_BLOCKS):
        cs = cb16[:, blk * RG_BLOCK:(blk + 1) * RG_BLOCK]
        ga.append(jnp.dot(cs, wa_ref[blk], preferred_element_type=F32))
        gx.append(jnp.dot(cs, wx_ref[blk], preferred_element_type=F32))
    r_gate = jax.nn.sigmoid(jnp.concatenate(ga, axis=1) + ba_ref[...])
    i_gate = jax.nn.sigmoid(jnp.concatenate(gx, axis=1) + bx_ref[...])
    log_a = -RG_C * r_gate * sp_ref[...]
    a_scr[...] = jnp.exp(log_a)
    th = jnp.tanh(log_a)
    b_scr[...] = jnp.sqrt(-2.0 * th / (1.0 - th)) * (i_gate * c)

    def step(k, h):
        t = (steps - 1 - k) if reverse else k
        row = pl.multiple_of(t * batch, batch)
        h = a_scr[pl.ds(row, batch), :] * h + b_scr[pl.ds(row, batch), :]
        b_scr[pl.ds(row, batch), :] = h
        return h

    h_scr[...] = lax.fori_loop(0, steps, step, h_scr[...], unroll=4)

    if not final:
        out_ref[...] = b_scr[...]
        return

    h_tot = b_scr[...] + hprev_ref[...]
    g = _mm(x, wg_ref[...])
    y = h_tot * _gelu(g)
    mix = _mm(y, wo_ref[...])
    out_ref[...] = _layernorm(ALPHA * x + mix, lng_ref[...], lnb_ref[...])


def _rg_call(x, wr, cw, cb, wa, ba, wx, bx, sp, batch, reverse, tail):
    rows = x.shape[0]
    chunk = math.gcd(rows, RG_ROWS)
    n = rows // chunk
    steps = chunk // batch
    final = tail is not None
    per_prev = chunk // batch
    per_next = chunk // (2 * batch)
    last_next = rows // (2 * batch) - 1

    def cidx(i):
        return (n - 1 - i) if reverse else i

    tok = pl.BlockSpec((chunk, D_MODEL), lambda i: (cidx(i), 0))
    prev = pl.BlockSpec((batch, D_MODEL), lambda i: (jnp.maximum(cidx(i) * per_prev - 1, 0), 0))
    nxt = pl.BlockSpec((2 * batch, D_MODEL), lambda i: (jnp.minimum((cidx(i) + 1) * per_next, last_next), 0))
    args = [x, x, x, wr, cw, cb, wa, ba, wx, bx, sp]
    specs = [tok, prev, nxt] + [_const_spec(a.shape) for a in args[3:]]
    if final:
        args += list(tail)
        specs += [tok] + [_const_spec(a.shape) for a in tail[1:]]
    return pl.pallas_call(
        functools.partial(_rg_kernel, batch=batch, steps=steps, reverse=reverse, final=final, n_chunks=n),
        grid=(n,),
        in_specs=specs,
        out_specs=tok,
        out_shape=jax.ShapeDtypeStruct((rows, D_MODEL), F32),
        scratch_shapes=[pltpu.VMEM((chunk + 3 * batch, D_MODEL), F32),
                        pltpu.VMEM((chunk, D_MODEL), F32),
                        pltpu.VMEM((chunk, D_MODEL), F32),
                        pltpu.VMEM((batch, D_MODEL), F32)],
        compiler_params=_cparams("arbitrary"),
        name="rg_bwd_out_ln" if final else "rg_fwd",
    )(*args)


def _rg_layer(x, batch, w_in, conv_w, conv_b, w_ga, b_ga, w_gx, b_gx, lam, w_out, ln_g, ln_b):
    wg = w_in[:, :D_MODEL].astype(BF16)
    wr = w_in[:, D_MODEL:].astype(BF16)
    cw = jnp.concatenate([conv_w.astype(F32), jnp.zeros((8 - RG_CONV, D_MODEL), F32)], axis=0)
    cb = conv_b.astype(F32)[None, :]
    sp = jax.nn.softplus(-lam.astype(F32))
    common = lambda d: (wr, cw, cb, w_ga[d].astype(BF16), b_ga[d][None, :].astype(F32),
                        w_gx[d].astype(BF16), b_gx[d][None, :].astype(F32), sp[d][None, :])
    h_fwd = _rg_call(x, *common(0), batch, False, None)
    tail = (h_fwd, wg, w_out.astype(BF16), ln_g[None, :], ln_b[None, :])
    return _rg_call(x, *common(1), batch, True, tail)


def _sorting_pairs(n):
    pairs = []
    p = 1
    while p < n:
        k = p
        while k >= 1:
            for j in range(k % p, n - k, 2 * k):
                for i in range(min(k, n - j - k)):
                    if (i + j) // (2 * p) == (i + j + k) // (2 * p):
                        pairs.append((i + j, i + j + k))
            k //= 2
        p *= 2
    return pairs


_SORT16 = _sorting_pairs(16)
_SORT8 = _sorting_pairs(8)


def _exchange(x, i, j):
    x[i], x[j] = jnp.maximum(x[i], x[j]), jnp.minimum(x[i], x[j])


def _bitonic_merge(x):
    d = len(x) // 2
    while d >= 1:
        for i in range(len(x)):
            if i & d == 0:
                _exchange(x, i, i + d)
        d //= 2


def _merge_top16_across_sublanes(x, shifts):
    for shift in shifts:
        y = [pltpu.roll(v, shift, 0) for v in x]
        x = [jnp.maximum(x[k], y[PEER_TOPK - 1 - k]) for k in range(PEER_TOPK)]
        _bitonic_merge(x)
    return x


def _count_ge(tiles, thr):
    cnt = None
    for t in tiles:
        c = jnp.where(t >= thr, 1.0, 0.0)
        cnt = c if cnt is None else cnt + c
    for shift in (4, 2, 1):
        cnt = cnt + pltpu.roll(cnt, shift, 0)
    return cnt


def _peer_topk_fast(sk_ref, an_ref, br_ref, qt_scr, flag_ref, tq):
    neg_inf = jnp.float32(-jnp.inf)
    sub_iota = lax.broadcasted_iota(jnp.int32, (8, LANES), 0)
    sub_is = [sub_iota == r for r in range(8)]

    def head_body(h, carry):
        bad = jnp.zeros((8, LANES), F32)
        for lt in range(tq // LANES):
            lanes = slice(lt * LANES, (lt + 1) * LANES)
            tiles = []
            tops = []
            for c in range(2):
                row0 = pl.multiple_of((2 * h + c) * PEER_HALF, PEER_HALF)
                q = qt_scr[pl.ds(row0, PEER_HALF), lanes]
                s = jnp.dot(sk_ref[c], q.astype(BF16), preferred_element_type=F32)
                t = [s[v * 8:(v + 1) * 8, :] for v in range(PEER_NKEYS // 8)]
                x = list(t)
                for i, j in _SORT16:
                    _exchange(x, i, j)
                tiles.append(t)
                tops.append(_merge_top16_across_sublanes(x, (4, 2, 1)))
            v0, v1 = tops

            cand = [[v0[a] + v1[b] for b in range(_PAIR_COUNTS[a])] for a in range(PEER_TOPK)]
            flat = [cand[a][b] for a in range(PEER_TOPK) for b in range(_PAIR_COUNTS[a])]
            packed = []
            for v in range(_PAIR_ROWS // 8):
                p = jnp.full((8, LANES), neg_inf, F32)
                for r in range(8):
                    if v * 8 + r < _N_PAIRS:
                        p = jnp.where(sub_is[r], flat[v * 8 + r], p)
                packed.append(p)
            x = packed + [jnp.full((8, LANES), neg_inf, F32)]
            for i, j in _SORT8:
                _exchange(x, i, j)
            y = [pltpu.roll(v, 4, 0) for v in x]
            x = x + y[::-1]
            _bitonic_merge(x)
            x = _merge_top16_across_sublanes(x, (2, 1))
            tau = x[PEER_TOPK - 1]
            bad = bad + jnp.where(_count_ge(packed, tau) != float(PEER_TOPK), 1.0, 0.0)

            top = flat[0]
            z = None
            for p in packed:
                e = jnp.where(p >= tau, jnp.exp(p - top), 0.0)
                z = e if z is None else z + e
            for shift in (4, 2, 1):
                z = z + pltpu.roll(z, shift, 0)
            inv_z = 1.0 / z

            counts = []
            for a in range(PEER_TOPK):
                n_a = None
                for b in range(_PAIR_COUNTS[a]):
                    c = jnp.where(cand[a][b] >= tau, 1.0, 0.0)
                    n_a = c if n_a is None else n_a + c
                counts.append(n_a)

            over0 = _count_ge(tiles[0], v0[PEER_TOPK - 1]) > float(PEER_TOPK)
            over1 = _count_ge(tiles[1], v1[PEER_TOPK - 1]) > float(PEER_TOPK)
            bad = bad + jnp.where(over0, counts[PEER_TOPK - 1], 0.0)
            for a in range(PEER_TOPK - 1):
                bad = bad + jnp.where(v0[a] == v0[a + 1], jnp.abs(counts[a] - counts[a + 1]), 0.0)
            for a in range(PEER_TOPK):
                edge = _PAIR_COUNTS[a]
                full = counts[a] == float(edge)
                if edge < PEER_TOPK:
                    tie = v1[edge - 1] == v1[edge]
                else:
                    tie = over1
                bad = bad + jnp.where(tie, jnp.where(full, 1.0, 0.0), 0.0)

            for v in range(PEER_NKEYS // 8):
                rows_ = slice(v * 8, (v + 1) * 8)
                s0 = tiles[0][v]
                s1 = tiles[1][v]
                n_dense = jnp.zeros((8, LANES), F32)
                rank1 = jnp.zeros((8, LANES), F32)
                for a in range(PEER_TOPK):
                    n_dense = jnp.where(s0 == v0[a], counts[a], n_dense)
                    rank1 = rank1 + jnp.where(v1[a] > s1, 1.0, 0.0)
                an_ref[0, h, rows_, lanes] = jnp.exp(s0 - v0[0]) * inv_z
                an_ref[1, h, rows_, lanes] = n_dense
                br_ref[0, h, rows_, lanes] = jnp.exp(s1 - v1[0]).astype(BF16)
                br_ref[1, h, rows_, lanes] = rank1.astype(BF16)
        flag_ref[h] = jnp.max(bad)
        return carry

    lax.fori_loop(0, PEER_HEADS, head_body, 0)


def _peer_topk_exact(head, sk_ref, an_ref, br_ref, qt_scr, vals_scr, rank_scr, e_scr, cand_scr, sel_scr, tq):
    neg_inf = jnp.float32(-jnp.inf)
    n_lt = tq // LANES
    iota_k = lax.broadcasted_iota(jnp.int32, (PEER_NKEYS, LANES), 0).astype(F32)

    def half_body(hc, _):
        q = qt_scr[pl.ds(pl.multiple_of(hc * PEER_HALF, PEER_HALF), PEER_HALF), :]
        s = jnp.dot(sk_ref[hc % 2], q.astype(BF16), preferred_element_type=F32)
        for lt in range(n_lt):
            lanes = slice(lt * LANES, (lt + 1) * LANES)
            s_l = s[:, lanes]

            work = s_l
            rank = jnp.full((PEER_NKEYS, LANES), float(PEER_NKEYS), F32)
            for r in range(PEER_TOPK):
                m = jnp.max(work, axis=0, keepdims=True)
                pos = jnp.min(jnp.where(work == m, iota_k, float(PEER_NKEYS)), axis=0, keepdims=True)
                hit = iota_k == pos
                vals_scr[hc, r:r + 1, lanes] = m
                work = jnp.where(hit, neg_inf, work)
                rank = jnp.where(hit, float(r), rank)
            top = vals_scr[hc, 0:1, lanes]
            rank_scr[hc, :, lanes] = rank
            e_scr[hc, :, lanes] = jnp.where(rank < PEER_TOPK, jnp.exp(s_l - top), 0.0)
        return 0

    half_body(2 * head, 0)
    half_body(2 * head + 1, 0)

    iota_p = lax.broadcasted_iota(jnp.int32, (_PAIR_ROWS, LANES), 0).astype(F32)

    def head_body(h, _):
        for lt in range(n_lt):
            lanes = slice(lt * LANES, (lt + 1) * LANES)
            v0 = vals_scr[2 * h, :, lanes]
            v1 = vals_scr[2 * h + 1, :, lanes]
            cand_scr[...] = jnp.full((_PAIR_ROWS, LANES), neg_inf, F32)
            for a in range(PEER_TOPK):
                cand_scr[_PAIR_OFFS[a]:_PAIR_OFFS[a] + _PAIR_COUNTS[a], :] = (
                    v0[a:a + 1, :] + v1[0:_PAIR_COUNTS[a], :])
            cand = cand_scr[...]

            def extract(r, carry):
                work, sel = carry
                m = jnp.max(work, axis=0, keepdims=True)
                pos = jnp.min(jnp.where(work == m, iota_p, float(_PAIR_ROWS)), axis=0, keepdims=True)
                hit = iota_p == pos
                return jnp.where(hit, neg_inf, work), jnp.where(hit, 1.0, sel)

            _, sel = lax.fori_loop(0, PEER_TOPK, extract, (cand, jnp.zeros((_PAIR_ROWS, LANES), F32)))
            sel_scr[...] = sel
            top = cand_scr[0:1, :]
            z = jnp.sum(jnp.where(sel > 0.0, jnp.exp(cand - top), 0.0), axis=0, keepdims=True)
            inv_z = 1.0 / z
            rank0 = rank_scr[2 * h, :, lanes]
            n_dense = jnp.zeros((PEER_NKEYS, LANES), F32)
            for a in range(PEER_TOPK):
                n_a = jnp.sum(sel_scr[_PAIR_OFFS[a]:_PAIR_OFFS[a] + _PAIR_COUNTS[a], :], axis=0, keepdims=True)
                n_dense = jnp.where(rank0 == float(a), n_a, n_dense)
            an_ref[0, h, :, lanes] = e_scr[2 * h, :, lanes] * inv_z
            an_ref[1, h, :, lanes] = n_dense
            br_ref[0, h, :, lanes] = e_scr[2 * h + 1, :, lanes].astype(BF16)
            br_ref[1, h, :, lanes] = rank_scr[2 * h + 1, :, lanes].astype(BF16)
        return 0

    head_body(head, 0)


def _peer_topk_kernel(x_ref, wqt_ref, sk_ref, an_ref, br_ref, qt_scr, vals_scr, rank_scr, e_scr, cand_scr,
                      sel_scr, flag_ref, *, tq):
    qt_scr[...] = lax.dot_general(wqt_ref[...], x_ref[...].astype(BF16), (((1,), (1,)), ((), ())),
                                  preferred_element_type=F32)
    _peer_topk_fast(sk_ref, an_ref, br_ref, qt_scr, flag_ref, tq)

    def redo(h, carry):
        @pl.when(flag_ref[h] > 0.0)
        def _():
            _peer_topk_exact(h, sk_ref, an_ref, br_ref, qt_scr, vals_scr, rank_scr, e_scr, cand_scr, sel_scr, tq)
        return carry

    lax.fori_loop(0, PEER_HEADS, redo, 0)


def _peer_topk(x, wqt, sk, tq):
    rows = x.shape[0]
    return pl.pallas_call(
        functools.partial(_peer_topk_kernel, tq=tq),
        grid=(rows // tq,),
        in_specs=[pl.BlockSpec((tq, D_MODEL), lambda i: (i, 0)), _const_spec(wqt.shape), _const_spec(sk.shape)],
        out_specs=[pl.BlockSpec((2, PEER_HEADS, PEER_NKEYS, tq), lambda i: (0, 0, 0, i)),
                   pl.BlockSpec((2, PEER_HEADS, PEER_NKEYS, tq), lambda i: (0, 0, 0, i))],
        out_shape=[jax.ShapeDtypeStruct((2, PEER_HEADS, PEER_NKEYS, rows), F32),
                   jax.ShapeDtypeStruct((2, PEER_HEADS, PEER_NKEYS, rows), BF16)],
        scratch_shapes=[pltpu.VMEM((2 * PEER_HEADS * PEER_HALF, tq), F32),
                        pltpu.VMEM((2 * PEER_HEADS, PEER_TOPK, tq), F32),
                        pltpu.VMEM((2 * PEER_HEADS, PEER_NKEYS, tq), F32),
                        pltpu.VMEM((2 * PEER_HEADS, PEER_NKEYS, tq), F32),
                        pltpu.VMEM((_PAIR_ROWS, LANES), F32),
                        pltpu.VMEM((_PAIR_ROWS, LANES), F32),
                        pltpu.SMEM((PEER_HEADS,), F32)],
        compiler_params=_cparams("arbitrary"),
        name="peer_topk",
    )(x, wqt, sk)


PEER_CHUNK = 8 * PEER_NKEYS


PEER_STEP_CHUNKS = 2


def _peer_dense_kernel(x_ref, an_ref, br_ref, u_ref, vt_ref, p_ref, lng_ref, lnb_ref, wp_ref, wg_ref, o_ref,
                       xt_scr, acc_scr, br_scr, act_scr, wt_scr, *, tq):
    e = pl.program_id(1)

    @pl.when(e == 0)
    def _():
        xt_scr[...] = x_ref[...].T.astype(BF16)
        acc_scr[...] = jnp.zeros_like(acc_scr)
        br_scr[...] = br_ref[...]

    def up(q):
        rows_ = slice(2 * q * PEER_NKEYS, 2 * (q + 1) * PEER_NKEYS)
        act_scr[rows_, :] = jnp.dot(u_ref[rows_, :], xt_scr[...], preferred_element_type=F32)

    def gate_key(ii):
        rows_ = slice(ii * PEER_NKEYS, (ii + 1) * PEER_NKEYS)
        for lt in range(tq // LANES):
            lanes = slice(lt * LANES, (lt + 1) * LANES)
            gate = None
            for h in range(PEER_HEADS):
                a = jnp.broadcast_to(an_ref[0, h, ii:ii + 1, lanes], (PEER_NKEYS, LANES)).astype(BF16)
                n = jnp.broadcast_to(an_ref[1, h, ii:ii + 1, lanes], (PEER_NKEYS, LANES)).astype(BF16)
                term = a * jnp.where(br_scr[1, h, :, lanes] < n, br_scr[0, h, :, lanes], 0)
                gate = term if gate is None else gate + term
            wt_scr[rows_, lanes] = gate * _gelu(act_scr[rows_, lanes]).astype(BF16)

    def down(c):
        cols = slice(c * PEER_CHUNK, (c + 1) * PEER_CHUNK)
        acc_scr[...] += jnp.dot(vt_ref[:, cols], wt_scr[cols, :], preferred_element_type=F32)

    n_up = 4 * PEER_STEP_CHUNKS
    up(0)
    for q in range(n_up):
        if q + 1 < n_up:
            up(q + 1)
        gate_key(2 * q)
        gate_key(2 * q + 1)
        if q % 4 == 3:
            down(q // 4)

    @pl.when(e == pl.num_programs(1) - 1)
    def _():
        x2 = _layernorm(ALPHA * x_ref[...] + acc_scr[...].T, lng_ref[...], lnb_ref[...])
        o_ref[...] = x2 + _mm(p_ref[...], wp_ref[...]) * jax.nn.sigmoid(_mm(x2, wg_ref[...]))


def _peer_dense(x, an, br, u16, vt16, p, lng, lnb, wp, wg, tq):
    rows = x.shape[0]
    step = PEER_STEP_CHUNKS * PEER_CHUNK
    return pl.pallas_call(
        functools.partial(_peer_dense_kernel, tq=tq),
        grid=(rows // tq, PEER_EXPERTS // step),
        in_specs=[pl.BlockSpec((tq, D_MODEL), lambda t, e: (t, 0)),
                  pl.BlockSpec((2, PEER_HEADS, 8 * PEER_STEP_CHUNKS, tq), lambda t, e: (0, 0, e, t)),
                  pl.BlockSpec((2, PEER_HEADS, PEER_NKEYS, tq), lambda t, e: (0, 0, 0, t)),
                  pl.BlockSpec((step, D_MODEL), lambda t, e: (e, 0)),
                  pl.BlockSpec((D_MODEL, step), lambda t, e: (0, e)),
                  pl.BlockSpec((tq, PLE_DIM), lambda t, e: (t, 0)),
                  _const_spec(lng.shape), _const_spec(lnb.shape), _const_spec(wp.shape), _const_spec(wg.shape)],
        out_specs=pl.BlockSpec((tq, D_MODEL), lambda t, e: (t, 0)),
        out_shape=jax.ShapeDtypeStruct((rows, D_MODEL), F32),
        scratch_shapes=[pltpu.VMEM((D_MODEL, tq), BF16),
                        pltpu.VMEM((D_MODEL, tq), F32),
                        pltpu.VMEM((2, PEER_HEADS, PEER_NKEYS, tq), BF16),
                        pltpu.VMEM((step, tq), F32),
                        pltpu.VMEM((step, tq), BF16)],
        compiler_params=_cparams("arbitrary", "arbitrary"),
        name="peer_dense_ln_ple",
    )(x, an, br, u16, vt16, p, lng, lnb, wp, wg)


def _token_tile(rows, want):
    return math.gcd(rows, want)


def _trunk(x, p, s5, rg, ln, peer, ple):
    batch, seq, _ = x.shape
    rows = batch * seq
    ln1_g, ln1_b, ln2_g, ln2_b = ln
    peer_w_q, peer_subkeys, peer_u, peer_v = peer
    ple_w_proj, ple_w_gate = ple
    xt = jnp.transpose(x, (1, 0, 2)).reshape(rows, D_MODEL)
    pt = jnp.transpose(p, (0, 2, 1, 3)).reshape(DEPTH, rows, PLE_DIM)
    for i in range(DEPTH):
        j = i // 2
        if i % 2 == 0:
            xt = _s5_layer(xt, batch, *(w[j] for w in s5), ln1_g[i], ln1_b[i])
        else:
            xt = _rg_layer(xt, batch, *(w[j] for w in rg), ln1_g[i], ln1_b[i])
        an, br = _peer_topk(xt, peer_w_q[i].T.astype(BF16), peer_subkeys[i].astype(BF16), _token_tile(rows, 512))
        xt = _peer_dense(xt, an, br, peer_u[i].astype(BF16), peer_v[i].T.astype(BF16), pt[i],
                         ln2_g[i][None, :], ln2_b[i][None, :], ple_w_proj[i].astype(BF16),
                         ple_w_gate[i].astype(BF16), _token_tile(rows, 512))
    return jnp.transpose(xt.reshape(seq, batch, D_MODEL), (1, 0, 2))


def kernel(x_prompt, x_sample, p_prompt, p_sample, s5_w_in, s5_lam_re, s5_lam_im, s5_log_step, s5_b_re, s5_b_im, s5_c_re, s5_c_im, s5_d, s5_w_glu, rg_w_in, rg_conv_w, rg_conv_b, rg_w_gate_a, rg_b_gate_a, rg_w_gate_x, rg_b_gate_x, rg_lambda, rg_w_out, ln1_g, ln1_b, ln2_g, ln2_b, peer_w_q, peer_subkeys, peer_u, peer_v, ple_w_proj, ple_w_gate):
    s5 = (s5_w_in, s5_lam_re, s5_lam_im, s5_log_step, s5_b_re, s5_b_im, s5_c_re, s5_c_im, s5_d, s5_w_glu)
    rg = (rg_w_in, rg_conv_w, rg_conv_b, rg_w_gate_a, rg_b_gate_a, rg_w_gate_x, rg_b_gate_x, rg_lambda, rg_w_out)
    ln = (ln1_g, ln1_b, ln2_g, ln2_b)
    peer = (peer_w_q, peer_subkeys, peer_u, peer_v)
    ple = (ple_w_proj, ple_w_gate)
    y_prompt = _trunk(x_prompt, p_prompt, s5, rg, ln, peer, ple)
    y_sample = _trunk(x_sample, p_sample, s5, rg, ln, peer, ple)
    return (y_prompt, y_sample)
```

```python
import functools
import math

import jax
import jax.numpy as jnp
from jax import lax
from jax.experimental import pallas as pl
from jax.experimental.pallas import tpu as pltpu

F32 = jnp.float32
BF16 = jnp.bfloat16

D_MODEL = 1024
DEPTH = 2
S5_GROUP = 16
S5_GROUPS = 64
S5_STATE = 64
S5_SLABS = 4
S5_SLAB_CH = D_MODEL // S5_SLABS
S5_SLAB_ST = 16 * S5_STATE
RG_BLOCKS = 4
RG_BLOCK = 256
RG_CONV = 4
RG_C = 8.0
PEER_HEADS = 8
PEER_NKEYS = 128
PEER_EXPERTS = PEER_NKEYS * PEER_NKEYS
PEER_HALF = 128
PEER_TOPK = 16
PLE_DIM = 256
ALPHA = (2 * DEPTH) ** 0.25
LN_EPS = 1e-5

LANES = 128
VMEM_LIMIT = 56 * 1024 * 1024
MIX_ROWS = 512
RG_ROWS = 1024
SCAN_COLS = 512

_PAIR_COUNTS = [min(PEER_TOPK, PEER_TOPK // (a + 1)) for a in range(PEER_TOPK)]
_PAIR_OFFS = [sum(_PAIR_COUNTS[:a]) for a in range(PEER_TOPK)]
_N_PAIRS = sum(_PAIR_COUNTS)
_PAIR_ROWS = ((_N_PAIRS + 7) // 8) * 8


def _mm(a, b):
    return jnp.dot(a.astype(BF16), b.astype(BF16), preferred_element_type=F32)


def _gelu(x):
    k = math.sqrt(2.0 / math.pi)
    hx = 0.5 * x
    return hx + hx * jnp.tanh(x * (k + (k * 0.044715) * (x * x)))


def _layernorm(z, g, b):
    mu = jnp.mean(z, axis=-1, keepdims=True)
    zc = z - mu
    var = jnp.mean(zc * zc, axis=-1, keepdims=True)
    return zc * lax.rsqrt(var + LN_EPS) * g + b


def _cparams(*sem):
    return pltpu.CompilerParams(dimension_semantics=sem, vmem_limit_bytes=VMEM_LIMIT)


def _const_spec(shape):
    nd = len(shape)
    return pl.BlockSpec(shape, lambda *_: (0,) * nd)


def _s5_kernel(*refs, batch, steps, reverse, final):
    if final:
        (x_ref, win_ref, wb_ref, wc_ref, a_ref, yprev_ref, d_ref, wglu_ref, lng_ref, lnb_ref,
         out_ref, bu_scr, h_scr) = refs
    else:
        x_ref, win_ref, wb_ref, wc_ref, a_ref, out_ref, bu_scr, h_scr = refs

    @pl.when(pl.program_id(0) == 0)
    def _():
        h_scr[...] = jnp.zeros_like(h_scr)

    x = x_ref[...]
    u = _mm(x, win_ref[...])
    ub = u.astype(BF16)
    for s in range(S5_SLABS):
        bu_scr[:, s * 2 * S5_SLAB_ST:(s + 1) * 2 * S5_SLAB_ST] = jnp.dot(
            ub[:, s * S5_SLAB_CH:(s + 1) * S5_SLAB_CH], wb_ref[s], preferred_element_type=F32)

    for s in range(S5_SLABS):
        for cb in range(S5_SLAB_ST // SCAN_COLS):
            c_re = s * 2 * S5_SLAB_ST + cb * SCAN_COLS
            c_im = c_re + S5_SLAB_ST
            a_re = jnp.broadcast_to(a_ref[s:s + 1, cb * SCAN_COLS:(cb + 1) * SCAN_COLS], (batch, SCAN_COLS))
            a_im = jnp.broadcast_to(
                a_ref[S5_SLABS + s:S5_SLABS + s + 1, cb * SCAN_COLS:(cb + 1) * SCAN_COLS], (batch, SCAN_COLS))

            def step(k, carry, c_re=c_re, c_im=c_im, a_re=a_re, a_im=a_im):
                h_re, h_im = carry
                t = (steps - 1 - k) if reverse else k
                row = pl.multiple_of(t * batch, batch)
                b_re = bu_scr[pl.ds(row, batch), c_re:c_re + SCAN_COLS]
                b_im = bu_scr[pl.ds(row, batch), c_im:c_im + SCAN_COLS]
                n_re = a_re * h_re - a_im * h_im + b_re
                n_im = a_re * h_im + a_im * h_re + b_im
                bu_scr[pl.ds(row, batch), c_re:c_re + SCAN_COLS] = n_re
                bu_scr[pl.ds(row, batch), c_im:c_im + SCAN_COLS] = n_im
                return n_re, n_im

            h_re, h_im = lax.fori_loop(
                0, steps, step,
                (h_scr[:, c_re:c_re + SCAN_COLS], h_scr[:, c_im:c_im + SCAN_COLS]), unroll=4)
            h_scr[:, c_re:c_re + SCAN_COLS] = h_re
            h_scr[:, c_im:c_im + SCAN_COLS] = h_im

    ys = []
    for s in range(S5_SLABS):
        hb = bu_scr[:, s * 2 * S5_SLAB_ST:(s + 1) * 2 * S5_SLAB_ST].astype(BF16)
        ys.append(jnp.dot(hb, wc_ref[s], preferred_element_type=F32))
    y = jnp.concatenate(ys, axis=1)

    if not final:
        out_ref[...] = y
        return

    y = y + yprev_ref[...] + d_ref[...] * u
    hg = _gelu(y)
    vg = _mm(hg, wglu_ref[...])
    mix = vg[:, :D_MODEL] * jax.nn.sigmoid(vg[:, D_MODEL:])
    out_ref[...] = _layernorm(ALPHA * x + mix, lng_ref[...], lnb_ref[...])


def _s5_discretize(lam_re, lam_im, log_step, b_re, b_im):
    step = jnp.exp(log_step.astype(F32))[:, None]
    lr = lam_re.astype(F32)
    li = lam_im.astype(F32)
    mag = jnp.exp(lr * step)
    ang = li * step
    ar = mag * jnp.cos(ang)
    ai = mag * jnp.sin(ang)
    den = lr * lr + li * li
    zr = ar - 1.0
    qr = (zr * lr + ai * li) / den
    qi = (ai * lr - zr * li) / den
    br = b_re.astype(F32)
    bi = b_im.astype(F32)
    bbr = qr[..., None] * br - qi[..., None] * bi
    bbi = qr[..., None] * bi + qi[..., None] * br
    return ar, ai, bbr, bbi


def _s5_weights(lam_re, lam_im, log_step, b_re, b_im, c_re, c_im):
    ar, ai, bbr, bbi = _s5_discretize(lam_re, lam_im, log_step, b_re, b_im)
    eye = jnp.eye(16, dtype=F32)

    def in_blocks(bb):
        bb4 = bb.reshape(S5_SLABS, 16, S5_STATE, S5_GROUP)
        return jnp.einsum('sgpc,gh->sgchp', bb4, eye).reshape(S5_SLABS, S5_SLAB_CH, S5_SLAB_ST)

    def out_blocks(cc):
        cc4 = cc.astype(F32).reshape(S5_SLABS, 16, S5_GROUP, S5_STATE)
        return jnp.einsum('sgcp,gh->sgphc', cc4, eye).reshape(S5_SLABS, S5_SLAB_ST, S5_SLAB_CH)

    wb = jnp.concatenate([in_blocks(bbr), in_blocks(bbi)], axis=2).astype(BF16)
    wc = jnp.concatenate([out_blocks(c_re), -out_blocks(c_im)], axis=1).astype(BF16)
    avec = jnp.concatenate([ar.reshape(S5_SLABS, S5_SLAB_ST), ai.reshape(S5_SLABS, S5_SLAB_ST)], axis=0)
    return wb, wc, avec


def _s5_call(x, win, wb, wc, avec, batch, reverse, tail):
    rows = x.shape[0]
    n = rows // MIX_ROWS
    steps = MIX_ROWS // batch
    final = tail is not None
    if reverse:
        tok = pl.BlockSpec((MIX_ROWS, D_MODEL), lambda i: (n - 1 - i, 0))
    else:
        tok = pl.BlockSpec((MIX_ROWS, D_MODEL), lambda i: (i, 0))
    args = [x, win, wb, wc, avec]
    specs = [tok, _const_spec(win.shape), _const_spec(wb.shape), _const_spec(wc.shape), _const_spec(avec.shape)]
    if final:
        yprev, dskip, wglu, lng, lnb = tail
        args += [yprev, dskip, wglu, lng, lnb]
        specs += [tok, _const_spec(dskip.shape), _const_spec(wglu.shape), _const_spec(lng.shape),
                  _const_spec(lnb.shape)]
    return pl.pallas_call(
        functools.partial(_s5_kernel, batch=batch, steps=steps, reverse=reverse, final=final),
        grid=(n,),
        in_specs=specs,
        out_specs=tok,
        out_shape=jax.ShapeDtypeStruct((rows, D_MODEL), F32),
        scratch_shapes=[pltpu.VMEM((MIX_ROWS, S5_SLABS * 2 * S5_SLAB_ST), F32),
                        pltpu.VMEM((batch, S5_SLABS * 2 * S5_SLAB_ST), F32)],
        compiler_params=_cparams("arbitrary"),
        name="s5_bwd_glu_ln" if final else "s5_fwd",
    )(*args)


def _s5_layer(x, batch, w_in, lam_re, lam_im, log_step, b_re, b_im, c_re, c_im, d_skip, w_glu, ln_g, ln_b):
    win = w_in.astype(BF16)
    wb0, wc0, av0 = _s5_weights(lam_re[0], lam_im[0], log_step[0], b_re[0], b_im[0], c_re[0], c_im[0])
    wb1, wc1, av1 = _s5_weights(lam_re[1], lam_im[1], log_step[1], b_re[1], b_im[1], c_re[1], c_im[1])
    y_fwd = _s5_call(x, win, wb0, wc0, av0, batch, False, None)
    tail = (y_fwd, d_skip.astype(F32)[None, :], w_glu.astype(BF16), ln_g[None, :], ln_b[None, :])
    return _s5_call(x, win, wb1, wc1, av1, batch, True, tail)


def _rg_kernel(*refs, batch, steps, reverse, final, n_chunks):
    if final:
        (x_ref, xp_ref, xn_ref, wr_ref, cw_ref, cb_ref, wa_ref, ba_ref, wx_ref, bx_ref, sp_ref,
         hprev_ref, wg_ref, wo_ref, lng_ref, lnb_ref, out_ref, xe_scr, a_scr, b_scr, h_scr) = refs
    else:
        (x_ref, xp_ref, xn_ref, wr_ref, cw_ref, cb_ref, wa_ref, ba_ref, wx_ref, bx_ref, sp_ref,
         out_ref, xe_scr, a_scr, b_scr, h_scr) = refs
    rows = steps * batch
    i = pl.program_id(0)
    chunk = (n_chunks - 1 - i) if reverse else i

    @pl.when(i == 0)
    def _():
        h_scr[...] = jnp.zeros_like(h_scr)

    x = x_ref[...]
    keep_prev = (chunk > 0).astype(F32)
    keep_next = (chunk < n_chunks - 1).astype(F32)
    xe_scr[0:batch, :] = xp_ref[...] * keep_prev
    xe_scr[batch:batch + rows, :] = x
    xe_scr[batch + rows:, :] = xn_ref[...] * keep_next
    r_ext = _mm(xe_scr[...], wr_ref[...])
    xe_scr[...] = r_ext

    c = cb_ref[...] + xe_scr[0:rows, :] * cw_ref[0:1, :]
    for k in range(1, RG_CONV):
        c = c + xe_scr[k * batch:k * batch + rows, :] * cw_ref[k:k + 1, :]

    cb16 = c.astype(BF16)
    ga = []
    gx = []
    for blk in range(RG_BLOCKS):
        cs = cb16[:, blk * RG_BLOCK:(blk + 1) * RG_BLOCK]
        ga.append(jnp.dot(cs, wa_ref[blk], preferred_element_type=F32))
        gx.append(jnp.dot(cs, wx_ref[blk], preferred_element_type=F32))
    r_gate = jax.nn.sigmoid(jnp.concatenate(ga, axis=1) + ba_ref[...])
    i_gate = jax.nn.sigmoid(jnp.concatenate(gx, axis=1) + bx_ref[...])
    log_a = -RG_C * r_gate * sp_ref[...]
    a_scr[...] = jnp.exp(log_a)
    th = jnp.tanh(log_a)
    b_scr[...] = jnp.sqrt(-2.0 * th / (1.0 - th)) * (i_gate * c)

    def step(k, h):
        t = (steps - 1 - k) if reverse else k
        row = pl.multiple_of(t * batch, batch)
        h = a_scr[pl.ds(row, batch), :] * h + b_scr[pl.ds(row, batch), :]
        b_scr[pl.ds(row, batch), :] = h
        return h

    h_scr[...] = lax.fori_loop(0, steps, step, h_scr[...], unroll=4)

    if not final:
        out_ref[...] = b_scr[...]
        return

    h_tot = b_scr[...] + hprev_ref[...]
    g = _mm(x, wg_ref[...])
    y = h_tot * _gelu(g)
    mix = _mm(y, wo_ref[...])
    out_ref[...] = _layernorm(ALPHA * x + mix, lng_ref[...], lnb_ref[...])


def _rg_call(x, wr, cw, cb, wa, ba, wx, bx, sp, batch, reverse, tail):
    rows = x.shape[0]
    chunk = math.gcd(rows, RG_ROWS)
    n = rows // chunk
    steps = chunk // batch
    final = tail is not None
    per_prev = chunk // batch
    per_next = chunk // (2 * batch)
    last_next = rows // (2 * batch) - 1

    def cidx(i):
        return (n - 1 - i) if reverse else i

    tok = pl.BlockSpec((chunk, D_MODEL), lambda i: (cidx(i), 0))
    prev = pl.BlockSpec((batch, D_MODEL), lambda i: (jnp.maximum(cidx(i) * per_prev - 1, 0), 0))
    nxt = pl.BlockSpec((2 * batch, D_MODEL), lambda i: (jnp.minimum((cidx(i) + 1) * per_next, last_next), 0))
    args = [x, x, x, wr, cw, cb, wa, ba, wx, bx, sp]
    specs = [tok, prev, nxt] + [_const_spec(a.shape) for a in args[3:]]
    if final:
        args += list(tail)
        specs += [tok] + [_const_spec(a.shape) for a in tail[1:]]
    return pl.pallas_call(
        functools.partial(_rg_kernel, batch=batch, steps=steps, reverse=reverse, final=final, n_chunks=n),
        grid=(n,),
        in_specs=specs,
        out_specs=tok,
        out_shape=jax.ShapeDtypeStruct((rows, D_MODEL), F32),
        scratch_shapes=[pltpu.VMEM((chunk + 3 * batch, D_MODEL), F32),
                        pltpu.VMEM((chunk, D_MODEL), F32),
                        pltpu.VMEM((chunk, D_MODEL), F32),
                        pltpu.VMEM((batch, D_MODEL), F32)],
        compiler_params=_cparams("arbitrary"),
        name="rg_bwd_out_ln" if final else "rg_fwd",
    )(*args)


def _rg_layer(x, batch, w_in, conv_w, conv_b, w_ga, b_ga, w_gx, b_gx, lam, w_out, ln_g, ln_b):
    wg = w_in[:, :D_MODEL].astype(BF16)
    wr = w_in[:, D_MODEL:].astype(BF16)
    cw = jnp.concatenate([conv_w.astype(F32), jnp.zeros((8 - RG_CONV, D_MODEL), F32)], axis=0)
    cb = conv_b.astype(F32)[None, :]
    sp = jax.nn.softplus(-lam.astype(F32))
    common = lambda d: (wr, cw, cb, w_ga[d].astype(BF16), b_ga[d][None, :].astype(F32),
                        w_gx[d].astype(BF16), b_gx[d][None, :].astype(F32), sp[d][None, :])
    h_fwd = _rg_call(x, *common(0), batch, False, None)
    tail = (h_fwd, wg, w_out.astype(BF16), ln_g[None, :], ln_b[None, :])
    return _rg_call(x, *common(1), batch, True, tail)


def _sorting_pairs(n):
    pairs = []
    p = 1
    while p < n:
        k = p
        while k >= 1:
            for j in range(k % p, n - k, 2 * k):
                for i in range(min(k, n - j - k)):
                    if (i + j) // (2 * p) == (i + j + k) // (2 * p):
                        pairs.append((i + j, i + j + k))
            k //= 2
        p *= 2
    return pairs


_SORT16 = _sorting_pairs(16)
_SORT8 = _sorting_pairs(8)


def _exchange(x, i, j):
    x[i], x[j] = jnp.maximum(x[i], x[j]), jnp.minimum(x[i], x[j])


def _bitonic_merge(x):
    d = len(x) // 2
    while d >= 1:
        for i in range(len(x)):
            if i & d == 0:
                _exchange(x, i, i + d)
        d //= 2


def _merge_top16_across_sublanes(x, shifts):
    for shift in shifts:
        y = [pltpu.roll(v, shift, 0) for v in x]
        x = [jnp.maximum(x[k], y[PEER_TOPK - 1 - k]) for k in range(PEER_TOPK)]
        _bitonic_merge(x)
    return x


def _count_ge(tiles, thr):
    cnt = None
    for t in tiles:
        c = jnp.where(t >= thr, 1.0, 0.0)
        cnt = c if cnt is None else cnt + c
    for shift in (4, 2, 1):
        cnt = cnt + pltpu.roll(cnt, shift, 0)
    return cnt


def _peer_topk_fast(sk_ref, an_ref, br_ref, qt_scr, flag_ref, tq):
    neg_inf = jnp.float32(-jnp.inf)
    sub_iota = lax.broadcasted_iota(jnp.int32, (8, LANES), 0)
    sub_is = [sub_iota == r for r in range(8)]

    def head_body(h, carry):
        bad = jnp.zeros((8, LANES), F32)
        for lt in range(tq // LANES):
            lanes = slice(lt * LANES, (lt + 1) * LANES)
            tiles = []
            tops = []
            for c in range(2):
                row0 = pl.multiple_of((2 * h + c) * PEER_HALF, PEER_HALF)
                q = qt_scr[pl.ds(row0, PEER_HALF), lanes]
                s = jnp.dot(sk_ref[c], q.astype(BF16), preferred_element_type=F32)
                t = [s[v * 8:(v + 1) * 8, :] for v in range(PEER_NKEYS // 8)]
                x = list(t)
                for i, j in _SORT16:
                    _exchange(x, i, j)
                tiles.append(t)
                tops.append(_merge_top16_across_sublanes(x, (4, 2, 1)))
            v0, v1 = tops

            cand = [[v0[a] + v1[b] for b in range(_PAIR_COUNTS[a])] for a in range(PEER_TOPK)]
            flat = [cand[a][b] for a in range(PEER_TOPK) for b in range(_PAIR_COUNTS[a])]
            packed = []
            for v in range(_PAIR_ROWS // 8):
                p = jnp.full((8, LANES), neg_inf, F32)
                for r in range(8):
                    if v * 8 + r < _N_PAIRS:
                        p = jnp.where(sub_is[r], flat[v * 8 + r], p)
                packed.append(p)
            x = packed + [jnp.full((8, LANES), neg_inf, F32)]
            for i, j in _SORT8:
                _exchange(x, i, j)
            y = [pltpu.roll(v, 4, 0) for v in x]
            x = x + y[::-1]
            _bitonic_merge(x)
            x = _merge_top16_across_sublanes(x, (2, 1))
            tau = x[PEER_TOPK - 1]
            bad = bad + jnp.where(_count_ge(packed, tau) != float(PEER_TOPK), 1.0, 0.0)

            top = flat[0]
            z = None
            for p in packed:
                e = jnp.where(p >= tau, jnp.exp(p - top), 0.0)
                z = e if z is None else z + e
            for shift in (4, 2, 1):
                z = z + pltpu.roll(z, shift, 0)
            inv_z = 1.0 / z

            counts = []
            for a in range(PEER_TOPK):
                n_a = None
                for b in range(_PAIR_COUNTS[a]):
                    c = jnp.where(cand[a][b] >= tau, 1.0, 0.0)
                    n_a = c if n_a is None else n_a + c
                counts.append(n_a)

            over0 = _count_ge(tiles[0], v0[PEER_TOPK - 1]) > float(PEER_TOPK)
            over1 = _count_ge(tiles[1], v1[PEER_TOPK - 1]) > float(PEER_TOPK)
            bad = bad + jnp.where(over0, counts[PEER_TOPK - 1], 0.0)
            for a in range(PEER_TOPK - 1):
                bad = bad + jnp.where(v0[a] == v0[a + 1], jnp.abs(counts[a] - counts[a + 1]), 0.0)
            for a in range(PEER_TOPK):
                edge = _PAIR_COUNTS[a]
                full = counts[a] == float(edge)
                if edge < PEER_TOPK:
                    tie = v1[edge - 1] == v1[edge]
                else:
                    tie = over1
                bad = bad + jnp.where(tie, jnp.where(full, 1.0, 0.0), 0.0)

            for v in range(PEER_NKEYS // 8):
                rows_ = slice(v * 8, (v + 1) * 8)
                s0 = tiles[0][v]
                s1 = tiles[1][v]
                n_dense = jnp.zeros((8, LANES), F32)
                rank1 = jnp.zeros((8, LANES), F32)
                for a in range(PEER_TOPK):
                    n_dense = jnp.where(s0 == v0[a], counts[a], n_dense)
                    rank1 = rank1 + jnp.where(v1[a] > s1, 1.0, 0.0)
                an_ref[0, h, rows_, lanes] = jnp.exp(s0 - v0[0]) * inv_z
                an_ref[1, h, rows_, lanes] = n_dense
                br_ref[0, h, rows_, lanes] = jnp.exp(s1 - v1[0]).astype(BF16)
                br_ref[1, h, rows_, lanes] = rank1.astype(BF16)
        flag_ref[h] = jnp.max(bad)
        return carry

    lax.fori_loop(0, PEER_HEADS, head_body, 0)


def _peer_topk_exact(head, sk_ref, an_ref, br_ref, qt_scr, vals_scr, rank_scr, e_scr, cand_scr, sel_scr, tq):
    neg_inf = jnp.float32(-jnp.inf)
    n_lt = tq // LANES
    iota_k = lax.broadcasted_iota(jnp.int32, (PEER_NKEYS, LANES), 0).astype(F32)

    def half_body(hc, _):
        q = qt_scr[pl.ds(pl.multiple_of(hc * PEER_HALF, PEER_HALF), PEER_HALF), :]
        s = jnp.dot(sk_ref[hc % 2], q.astype(BF16), preferred_element_type=F32)
        for lt in range(n_lt):
            lanes = slice(lt * LANES, (lt + 1) * LANES)
            s_l = s[:, lanes]

            work = s_l
            rank = jnp.full((PEER_NKEYS, LANES), float(PEER_NKEYS), F32)
            for r in range(PEER_TOPK):
                m = jnp.max(work, axis=0, keepdims=True)
                pos = jnp.min(jnp.where(work == m, iota_k, float(PEER_NKEYS)), axis=0, keepdims=True)
                hit = iota_k == pos
                vals_scr[hc, r:r + 1, lanes] = m
                work = jnp.where(hit, neg_inf, work)
                rank = jnp.where(hit, float(r), rank)
            top = vals_scr[hc, 0:1, lanes]
            rank_scr[hc, :, lanes] = rank
            e_scr[hc, :, lanes] = jnp.where(rank < PEER_TOPK, jnp.exp(s_l - top), 0.0)
        return 0

    half_body(2 * head, 0)
    half_body(2 * head + 1, 0)

    iota_p = lax.broadcasted_iota(jnp.int32, (_PAIR_ROWS, LANES), 0).astype(F32)

    def head_body(h, _):
        for lt in range(n_lt):
            lanes = slice(lt * LANES, (lt + 1) * LANES)
            v0 = vals_scr[2 * h, :, lanes]
            v1 = vals_scr[2 * h + 1, :, lanes]
            cand_scr[...] = jnp.full((_PAIR_ROWS, LANES), neg_inf, F32)
            for a in range(PEER_TOPK):
                cand_scr[_PAIR_OFFS[a]:_PAIR_OFFS[a] + _PAIR_COUNTS[a], :] = (
                    v0[a:a + 1, :] + v1[0:_PAIR_COUNTS[a], :])
            cand = cand_scr[...]

            def extract(r, carry):
                work, sel = carry
                m = jnp.max(work, axis=0, keepdims=True)
                pos = jnp.min(jnp.where(work == m, iota_p, float(_PAIR_ROWS)), axis=0, keepdims=True)
                hit = iota_p == pos
                return jnp.where(hit, neg_inf, work), jnp.where(hit, 1.0, sel)

            _, sel = lax.fori_loop(0, PEER_TOPK, extract, (cand, jnp.zeros((_PAIR_ROWS, LANES), F32)))
            sel_scr[...] = sel
            top = cand_scr[0:1, :]
            z = jnp.sum(jnp.where(sel > 0.0, jnp.exp(cand - top), 0.0), axis=0, keepdims=True)
            inv_z = 1.0 / z
            rank0 = rank_scr[2 * h, :, lanes]
            n_dense = jnp.zeros((PEER_NKEYS, LANES), F32)
            for a in range(PEER_TOPK):
                n_a = jnp.sum(sel_scr[_PAIR_OFFS[a]:_PAIR_OFFS[a] + _PAIR_COUNTS[a], :], axis=0, keepdims=True)
                n_dense = jnp.where(rank0 == float(a), n_a, n_dense)
            an_ref[0, h, :, lanes] = e_scr[2 * h, :, lanes] * inv_z
            an_ref[1, h, :, lanes] = n_dense
            br_ref[0, h, :, lanes] = e_scr[2 * h + 1, :, lanes].astype(BF16)
            br_ref[1, h, :, lanes] = rank_scr[2 * h + 1, :, lanes].astype(BF16)
        return 0

    head_body(head, 0)


def _peer_topk_kernel(x_ref, wqt_ref, sk_ref, an_ref, br_ref, qt_scr, vals_scr, rank_scr, e_scr, cand_scr,
                      sel_scr, flag_ref, *, tq):
    qt_scr[...] = lax.dot_general(wqt_ref[...], x_ref[...].astype(BF16), (((1,), (1,)), ((), ())),
                                  preferred_element_type=F32)
    _peer_topk_fast(sk_ref, an_ref, br_ref, qt_scr, flag_ref, tq)

    def redo(h, carry):
        @pl.when(flag_ref[h] > 0.0)
        def _():
            _peer_topk_exact(h, sk_ref, an_ref, br_ref, qt_scr, vals_scr, rank_scr, e_scr, cand_scr, sel_scr, tq)
        return carry

    lax.fori_loop(0, PEER_HEADS, redo, 0)


def _peer_topk(x, wqt, sk, tq):
    rows = x.shape[0]
    return pl.pallas_call(
        functools.partial(_peer_topk_kernel, tq=tq),
        grid=(rows // tq,),
        in_specs=[pl.BlockSpec((tq, D_MODEL), lambda i: (i, 0)), _const_spec(wqt.shape), _const_spec(sk.shape)],
        out_specs=[pl.BlockSpec((2, PEER_HEADS, PEER_NKEYS, tq), lambda i: (0, 0, 0, i)),
                   pl.BlockSpec((2, PEER_HEADS, PEER_NKEYS, tq), lambda i: (0, 0, 0, i))],
        out_shape=[jax.ShapeDtypeStruct((2, PEER_HEADS, PEER_NKEYS, rows), F32),
                   jax.ShapeDtypeStruct((2, PEER_HEADS, PEER_NKEYS, rows), BF16)],
        scratch_shapes=[pltpu.VMEM((2 * PEER_HEADS * PEER_HALF, tq), F32),
                        pltpu.VMEM((2 * PEER_HEADS, PEER_TOPK, tq), F32),
                        pltpu.VMEM((2 * PEER_HEADS, PEER_NKEYS, tq), F32),
                        pltpu.VMEM((2 * PEER_HEADS, PEER_NKEYS, tq), F32),
                        pltpu.VMEM((_PAIR_ROWS, LANES), F32),
                        pltpu.VMEM((_PAIR_ROWS, LANES), F32),
                        pltpu.SMEM((PEER_HEADS,), F32)],
        compiler_params=_cparams("arbitrary"),
        name="peer_topk",
    )(x, wqt, sk)


PEER_CHUNK = 8 * PEER_NKEYS


PEER_STEP_CHUNKS = 2


def _peer_dense_kernel(x_ref, an_ref, br_ref, u_ref, vt_ref, p_ref, lng_ref, lnb_ref, wp_ref, wg_ref, o_ref,
                       xt_scr, acc_scr, br_scr, act_scr, wt_scr, *, tq, out_batch):
    e = pl.program_id(1)

    @pl.when(e == 0)
    def _():
        xt_scr[...] = x_ref[...].T.astype(BF16)
        acc_scr[...] = jnp.zeros_like(acc_scr)
        br_scr[...] = br_ref[...]

    def up(q):
        rows_ = slice(2 * q * PEER_NKEYS, 2 * (q + 1) * PEER_NKEYS)
        act_scr[rows_, :] = jnp.dot(u_ref[rows_, :], xt_scr[...], preferred_element_type=F32)

    def gate_key(ii):
        rows_ = slice(ii * PEER_NKEYS, (ii + 1) * PEER_NKEYS)
        for lt in range(tq // LANES):
            lanes = slice(lt * LANES, (lt + 1) * LANES)
            gate = None
            for h in range(PEER_HEADS):
                a = jnp.broadcast_to(an_ref[0, h, ii:ii + 1, lanes], (PEER_NKEYS, LANES)).astype(BF16)
                n = jnp.broadcast_to(an_ref[1, h, ii:ii + 1, lanes], (PEER_NKEYS, LANES)).astype(BF16)
                term = a * jnp.where(br_scr[1, h, :, lanes] < n, br_scr[0, h, :, lanes], 0)
                gate = term if gate is None else gate + term
            wt_scr[rows_, lanes] = gate * _gelu(act_scr[rows_, lanes]).astype(BF16)

    def down(c):
        cols = slice(c * PEER_CHUNK, (c + 1) * PEER_CHUNK)
        acc_scr[...] += jnp.dot(vt_ref[:, cols], wt_scr[cols, :], preferred_element_type=F32)

    n_up = 4 * PEER_STEP_CHUNKS
    up(0)
    for q in range(n_up):
        if q + 1 < n_up:
            up(q + 1)
        gate_key(2 * q)
        gate_key(2 * q + 1)
        if q % 4 == 3:
            down(q // 4)

    @pl.when(e == pl.num_programs(1) - 1)
    def _():
        x2 = _layernorm(ALPHA * x_ref[...] + acc_scr[...].T, lng_ref[...], lnb_ref[...])
        x3 = x2 + _mm(p_ref[...], wp_ref[...]) * jax.nn.sigmoid(_mm(x2, wg_ref[...]))
        if out_batch is None:
            o_ref[...] = x3
        else:
            o_ref[...] = pltpu.einshape("tbd->btd", x3.reshape(tq // out_batch, out_batch, D_MODEL))


def _peer_dense(x, an, br, u16, vt16, p, lng, lnb, wp, wg, tq, out_batch=None):
    rows = x.shape[0]
    step = PEER_STEP_CHUNKS * PEER_CHUNK
    if out_batch is None:
        out_spec = pl.BlockSpec((tq, D_MODEL), lambda t, e: (t, 0))
        out_shape = jax.ShapeDtypeStruct((rows, D_MODEL), F32)
    else:
        out_spec = pl.BlockSpec((out_batch, tq // out_batch, D_MODEL), lambda t, e: (0, t, 0))
        out_shape = jax.ShapeDtypeStruct((out_batch, rows // out_batch, D_MODEL), F32)
    return pl.pallas_call(
        functools.partial(_peer_dense_kernel, tq=tq, out_batch=out_batch),
        grid=(rows // tq, PEER_EXPERTS // step),
        in_specs=[pl.BlockSpec((tq, D_MODEL), lambda t, e: (t, 0)),
                  pl.BlockSpec((2, PEER_HEADS, 8 * PEER_STEP_CHUNKS, tq), lambda t, e: (0, 0, e, t)),
                  pl.BlockSpec((2, PEER_HEADS, PEER_NKEYS, tq), lambda t, e: (0, 0, 0, t)),
                  pl.BlockSpec((step, D_MODEL), lambda t, e: (e, 0)),
                  pl.BlockSpec((D_MODEL, step), lambda t, e: (0, e)),
                  pl.BlockSpec((tq, PLE_DIM), lambda t, e: (t, 0)),
                  _const_spec(lng.shape), _const_spec(lnb.shape), _const_spec(wp.shape), _const_spec(wg.shape)],
        out_specs=out_spec,
        out_shape=out_shape,
        scratch_shapes=[pltpu.VMEM((D_MODEL, tq), BF16),
                        pltpu.VMEM((D_MODEL, tq), F32),
                        pltpu.VMEM((2, PEER_HEADS, PEER_NKEYS, tq), BF16),
                        pltpu.VMEM((step, tq), F32),
                        pltpu.VMEM((step, tq), BF16)],
        compiler_params=_cparams("arbitrary", "arbitrary"),
        name="peer_dense_ln_ple",
    )(x, an, br, u16, vt16, p, lng, lnb, wp, wg)


def _token_tile(rows, want):
    return math.gcd(rows, want)


def _trunk(x, p, s5, rg, ln, peer, ple):
    batch, seq, _ = x.shape
    rows = batch * seq
    ln1_g, ln1_b, ln2_g, ln2_b = ln
    peer_w_q, peer_subkeys, peer_u, peer_v = peer
    ple_w_proj, ple_w_gate = ple
    xt = jnp.transpose(x, (1, 0, 2)).reshape(rows, D_MODEL)
    pt = jnp.transpose(p, (0, 2, 1, 3)).reshape(DEPTH, rows, PLE_DIM)
    for i in range(DEPTH):
        j = i // 2
        if i % 2 == 0:
            xt = _s5_layer(xt, batch, *(w[j] for w in s5), ln1_g[i], ln1_b[i])
        else:
            xt = _rg_layer(xt, batch, *(w[j] for w in rg), ln1_g[i], ln1_b[i])
        an, br = _peer_topk(xt, peer_w_q[i].T.astype(BF16), peer_subkeys[i].astype(BF16), _token_tile(rows, 512))
        xt = _peer_dense(xt, an, br, peer_u[i].astype(BF16), peer_v[i].T.astype(BF16), pt[i],
                         ln2_g[i][None, :], ln2_b[i][None, :], ple_w_proj[i].astype(BF16),
                         ple_w_gate[i].astype(BF16), _token_tile(rows, 512),
                         out_batch=batch if i == DEPTH - 1 else None)
    return xt


def kernel(x_prompt, x_sample, p_prompt, p_sample, s5_w_in, s5_lam_re, s5_lam_im, s5_log_step, s5_b_re, s5_b_im, s5_c_re, s5_c_im, s5_d, s5_w_glu, rg_w_in, rg_conv_w, rg_conv_b, rg_w_gate_a, rg_b_gate_a, rg_w_gate_x, rg_b_gate_x, rg_lambda, rg_w_out, ln1_g, ln1_b, ln2_g, ln2_b, peer_w_q, peer_subkeys, peer_u, peer_v, ple_w_proj, ple_w_gate):
    s5 = (s5_w_in, s5_lam_re, s5_lam_im, s5_log_step, s5_b_re, s5_b_im, s5_c_re, s5_c_im, s5_d, s5_w_glu)
    rg = (rg_w_in, rg_conv_w, rg_conv_b, rg_w_gate_a, rg_b_gate_a, rg_w_gate_x, rg_b_gate_x, rg_lambda, rg_w_out)
    ln = (ln1_g, ln1_b, ln2_g, ln2_b)
    peer = (peer_w_q, peer_subkeys, peer_u, peer_v)
    ple = (ple_w_proj, ple_w_gate)
    y_prompt = _trunk(x_prompt, p_prompt, s5, rg, ln, peer, ple)
    y_sample = _trunk(x_sample, p_sample, s5, rg, ln, peer, ple)
    return (y_prompt, y_sample)
```
